```python
import jax, jax.numpy as jnp
from jax import lax
import numpy as np

D_MODEL = 1024
BATCH = 8
SEQ = 4096
DEPTH = 2

CHUNK = 64
EPS = 1e-6
N_EVEN = (DEPTH + 1) // 2
N_ODD = DEPTH // 2

POOL_WINDOWS = (2, 4, 8, 16)
N_POOL_GROUPS = len(POOL_WINDOWS)
POOL_GROUP_DIM = D_MODEL // 8
POOL_WIDTH = N_POOL_GROUPS * POOL_GROUP_DIM

ATT_HEADS = 8
ATT_HEAD_DIM = 64
ATT_WIDTH = ATT_HEADS * ATT_HEAD_DIM
LEFT_CHUNKS = 8
LEFT = LEFT_CHUNKS * CHUNK
BAND = (LEFT_CHUNKS + 1) * CHUNK
MAX_REL = 128
N_REL = 2 * MAX_REL + 1
ATT_SCALE = ATT_HEAD_DIM ** -0.5
AB_IN_WIDTH = POOL_WIDTH + 3 * ATT_WIDTH
AB_MIX_WIDTH = POOL_WIDTH + ATT_WIDTH

SGU_CHUNK = 128
SGU_HEADS = 8
SGU_WIDTH = D_MODEL
SGU_HEAD_DIM = SGU_WIDTH // SGU_HEADS

N_GROUPS = 4
EXPERTS_PER_GROUP = 4
N_EXPERTS = N_GROUPS * EXPERTS_PER_GROUP
TOP_K_IN_GROUP = 2
D_EXPERT = 256

kernel_name = "hybrid_pool_chunkattn_gmlp_hmoe"


def rmsnorm(x, g):
    xf = x.astype(jnp.float32)
    y = xf * lax.rsqrt(jnp.mean(xf * xf, axis=-1, keepdims=True) + EPS) * g.astype(jnp.float32)
    return y.astype(x.dtype)


def layernorm(x, g, b):
    xf = x.astype(jnp.float32)
    mu = jnp.mean(xf, axis=-1, keepdims=True)
    var = jnp.mean(jnp.square(xf - mu), axis=-1, keepdims=True)
    y = (xf - mu) * lax.rsqrt(var + EPS) * g.astype(jnp.float32) + b.astype(jnp.float32)
    return y.astype(x.dtype)


def pool_mixer(p, w_pool, pool_scale):
    b_, s_, _ = p.shape
    pf = p.astype(jnp.float32).reshape(b_, s_, N_POOL_GROUPS, POOL_GROUP_DIM)
    cs = jnp.cumsum(pf, axis=1)
    cs = jnp.concatenate([jnp.zeros_like(cs[:, :1]), cs], axis=1)
    t = jnp.arange(s_)
    means = []
    for gi, w in enumerate(POOL_WINDOWS):
        start = jnp.maximum(t + 1 - w, 0)
        cnt = (t + 1 - start).astype(jnp.float32)
        cs_g = cs[:, :, gi]
        means.append((cs_g[:, 1:] - cs_g[:, start]) / cnt[None, :, None])
    pooled = jnp.stack(means, axis=2) - pf
    y = jnp.einsum('bsgc,gcd->bsgd', pooled, w_pool.astype(jnp.float32))
    y = y * pool_scale.astype(jnp.float32).reshape(N_POOL_GROUPS, POOL_GROUP_DIM)
    return y.reshape(b_, s_, POOL_WIDTH).astype(p.dtype)


def chunk_attention(q, k, v, rel_bias):
    b_, s_, h_, d_ = q.shape
    nc = s_ // CHUNK
    kp = jnp.pad(k, ((0, 0), (LEFT, 0), (0, 0), (0, 0)))
    vp = jnp.pad(v, ((0, 0), (LEFT, 0), (0, 0), (0, 0)))
    q_blocks = q.reshape(b_, nc, CHUNK, h_, d_).transpose(1, 0, 2, 3, 4)
    rel = jnp.arange(BAND)[None, :] - LEFT - jnp.arange(CHUNK)[:, None]
    bias = rel_bias.astype(jnp.float32)[:, jnp.clip(rel, -MAX_REL, MAX_REL) + MAX_REL]
    band_off = jnp.arange(BAND) - LEFT

    def one_chunk(args):
        c, qb = args
        kb = lax.dynamic_slice_in_dim(kp, c * CHUNK, BAND, axis=1)
        vb = lax.dynamic_slice_in_dim(vp, c * CHUNK, BAND, axis=1)
        s = jnp.einsum('bqhd,bkhd->bhqk', qb, kb).astype(jnp.float32) * ATT_SCALE + bias[None]
        valid = (c * CHUNK + band_off) >= 0
        s = jnp.where(valid[None, None, None, :], s, -1e30)
        pr = jax.nn.softmax(s, axis=-1)
        return jnp.einsum('bhqk,bkhd->bqhd', pr.astype(vb.dtype), vb)

    o = lax.map(one_chunk, (jnp.arange(nc), q_blocks))
    return o.transpose(1, 0, 2, 3, 4).reshape(b_, s_, h_ * d_)


def pool_attn_mixer(h, w_in, w_pool, pool_scale, rel_bias, w_out):
    b_, s_, _ = h.shape
    z = h @ w_in
    p = z[..., :POOL_WIDTH]
    q, k, v = jnp.split(z[..., POOL_WIDTH:], 3, axis=-1)
    shp = (b_, s_, ATT_HEADS, ATT_HEAD_DIM)
    a_out = pool_mixer(p, w_pool, pool_scale)
    b_out = chunk_attention(q.reshape(shp), k.reshape(shp), v.reshape(shp), rel_bias)
    return jnp.concatenate([a_out.astype(h.dtype), b_out.astype(h.dtype)], axis=-1) @ w_out


def sgu_mixer(h, w_in, b_in, ln_g, ln_b, w_s, b_s, w_out):
    b_, s_, _ = h.shape
    z = jax.nn.gelu(h @ w_in + b_in, approximate=False)
    u, v = jnp.split(z, 2, axis=-1)
    v = layernorm(v, ln_g, ln_b)
    nch = s_ // SGU_CHUNK
    vc = v.reshape(b_, nch, SGU_CHUNK, SGU_HEADS, SGU_HEAD_DIM)
    causal = jnp.tril(jnp.ones((SGU_CHUNK, SGU_CHUNK), dtype=w_s.dtype))
    ws = w_s * causal[None]
    mixed = jnp.einsum('hts,bnshd->bnthd', ws, vc) + b_s.T[None, None, :, :, None]
    return (u * mixed.reshape(b_, s_, SGU_WIDTH)) @ w_out


def hier_moe(h, wg_router, bg_router, we_router, be_router, w_gate, w_up, w_down):
    b_, s_, d_ = h.shape
    t = h.reshape(-1, d_)
    n = t.shape[0]
    g_logits = (t @ wg_router).astype(jnp.float32) + bg_router.astype(jnp.float32)
    g_prob = jax.nn.softmax(g_logits, axis=-1)
    g_top, g_idx = lax.top_k(g_logits, 1)
    g_w = jnp.take_along_axis(g_prob, g_idx, axis=-1)[:, 0]
    e_logits = jnp.einsum('nd,gde->nge', t, we_router).astype(jnp.float32) + be_router.astype(jnp.float32)
    e_sel = jnp.take_along_axis(e_logits, g_idx[:, :, None], axis=1)[:, 0]
    e_top, e_idx = lax.top_k(e_sel, TOP_K_IN_GROUP)
    e_w = jax.nn.softmax(e_top, axis=-1)
    expert_id = g_idx * EXPERTS_PER_GROUP + e_idx
    combine = jnp.sum(jax.nn.one_hot(expert_id, N_EXPERTS, dtype=jnp.float32)
                      * (g_w[:, None] * e_w)[..., None], axis=1)
    hg = jnp.einsum('nd,edf->nef', t, w_gate)
    hu = jnp.einsum('nd,edf->nef', t, w_up)
    act = jax.nn.silu(hg) * hu * combine.astype(t.dtype)[..., None]
    out = jnp.einsum('nef,efd->nd', act, w_down)
    return out.reshape(b_, s_, d_)


def setup_inputs(seed: int = 0) -> dict:
    key = jax.random.key(seed)
    ks = jax.random.split(key, 24)
    nrm = lambda k, shp: jax.random.normal(k, shp, dtype=jnp.float32)
    D = D_MODEL
    return {
        "x": nrm(ks[0], (BATCH, SEQ, D)),
        "norm_mix_g": 1.0 + 0.1 * nrm(ks[1], (DEPTH, D)),
        "norm_ffn_g": 1.0 + 0.1 * nrm(ks[2], (DEPTH, D)),
        "final_norm_g": 1.0 + 0.1 * nrm(ks[3], (D,)),
        "ab_w_in": nrm(ks[4], (N_EVEN, D, AB_IN_WIDTH)) * D ** -0.5,
        "pool_w": nrm(ks[5], (N_EVEN, N_POOL_GROUPS, POOL_GROUP_DIM, POOL_GROUP_DIM)) * POOL_GROUP_DIM ** -0.5,
        "pool_scale": 1.0 + 0.1 * nrm(ks[6], (N_EVEN, POOL_WIDTH)),
        "att_rel_bias": 0.1 * nrm(ks[7], (N_EVEN, ATT_HEADS, N_REL)),
        "ab_w_out": nrm(ks[8], (N_EVEN, AB_MIX_WIDTH, D)) * AB_MIX_WIDTH ** -0.5,
        "sgu_w_in": nrm(ks[9], (N_ODD, D, 2 * SGU_WIDTH)) * D ** -0.5,
        "sgu_b_in": 0.02 * nrm(ks[10], (N_ODD, 2 * SGU_WIDTH)),
        "sgu_ln_g": 1.0 + 0.1 * nrm(ks[11], (N_ODD, SGU_WIDTH)),
        "sgu_ln_b": 0.02 * nrm(ks[12], (N_ODD, SGU_WIDTH)),
        "sgu_w_s": nrm(ks[13], (N_ODD, SGU_HEADS, SGU_CHUNK, SGU_CHUNK)) * SGU_CHUNK ** -0.5,
        "sgu_b_s": 1.0 + 0.1 * nrm(ks[14], (N_ODD, SGU_HEADS, SGU_CHUNK)),
        "sgu_w_out": nrm(ks[15], (N_ODD, SGU_WIDTH, D)) * SGU_WIDTH ** -0.5,
        "moe_wg_router": nrm(ks[16], (DEPTH, D, N_GROUPS)) * D ** -0.5,
        "moe_bg_router": 0.01 * nrm(ks[17], (DEPTH, N_GROUPS)),
        "moe_we_router": nrm(ks[18], (DEPTH, N_GROUPS, D, EXPERTS_PER_GROUP)) * D ** -0.5,
        "moe_be_router": 0.01 * nrm(ks[19], (DEPTH, N_GROUPS, EXPERTS_PER_GROUP)),
        "moe_w_gate": nrm(ks[20], (DEPTH, N_EXPERTS, D, D_EXPERT)) * D ** -0.5,
        "moe_w_up": nrm(ks[21], (DEPTH, N_EXPERTS, D, D_EXPERT)) * D ** -0.5,
        "moe_w_down": nrm(ks[22], (DEPTH, N_EXPERTS, D_EXPERT, D)) * D_EXPERT ** -0.5,
    }


def reference(x, norm_mix_g, norm_ffn_g, final_norm_g,
              ab_w_in, pool_w, pool_scale, att_rel_bias, ab_w_out,
              sgu_w_in, sgu_b_in, sgu_ln_g, sgu_ln_b, sgu_w_s, sgu_b_s, sgu_w_out,
              moe_wg_router, moe_bg_router, moe_we_router, moe_be_router,
              moe_w_gate, moe_w_up, moe_w_down):
    h = x
    for layer in range(DEPTH):
        i = layer // 2
        y = rmsnorm(h, norm_mix_g[layer])
        if layer % 2 == 0:
            y = pool_attn_mixer(y, ab_w_in[i], pool_w[i], pool_scale[i], att_rel_bias[i], ab_w_out[i])
        else:
            y = sgu_mixer(y, sgu_w_in[i], sgu_b_in[i], sgu_ln_g[i], sgu_ln_b[i],
                          sgu_w_s[i], sgu_b_s[i], sgu_w_out[i])
        h = h + y
        y = hier_moe(rmsnorm(h, norm_ffn_g[layer]), moe_wg_router[layer], moe_bg_router[layer],
                     moe_we_router[layer], moe_be_router[layer],
                     moe_w_gate[layer], moe_w_up[layer], moe_w_down[layer])
        h = h + y
    return rmsnorm(h, final_norm_g)
```

```python
import functools

import jax
import jax.numpy as jnp
from jax import lax
from jax.experimental import pallas as pl
from jax.experimental.pallas import tpu as pltpu

EPS = 1e-6
CHUNK = 64
POOL_WINDOWS = (2, 4, 8, 16)
POOL_GROUP_DIM = 128
POOL_WIDTH = 512
POOL_HALO = 16
ATT_HEADS = 8
ATT_HEAD_DIM = 64
ATT_WIDTH = 512
LEFT_CHUNKS = 8
LEFT = LEFT_CHUNKS * CHUNK
BAND = LEFT + CHUNK
MAX_REL = 128
ATT_SCALE = ATT_HEAD_DIM ** -0.5
SGU_CHUNK = 128
SGU_HEADS = 8
N_GROUPS = 4
EXPERTS_PER_GROUP = 4
N_EXPERTS = 16
D_EXPERT = 256
ROUTER_LANES = 128

VMEM_LIMIT = 56 * 1024 * 1024

F32 = jnp.float32
BF16 = jnp.bfloat16


def _rms(x, g):
    return x * lax.rsqrt(jnp.mean(x * x, axis=-1, keepdims=True) + EPS) * g


def _inproj_kernel(x_ref, g_ref, w_ref, p_ref, q_ref, k_ref, v_ref):
    i = pl.program_id(1)

    @pl.when(i == 0)
    def _():
        k_ref[...] = jnp.zeros_like(k_ref)
        v_ref[...] = jnp.zeros_like(v_ref)

    @pl.when(i > 0)
    def _():
        xn = _rms(x_ref[0], g_ref[...]).astype(BF16)
        z = jnp.dot(xn, w_ref[...], preferred_element_type=F32)
        p_ref[0] = z[:, :POOL_WIDTH]
        q_ref[0] = (z[:, POOL_WIDTH:POOL_WIDTH + ATT_WIDTH] * ATT_SCALE).astype(BF16)
        k_ref[0] = z[:, POOL_WIDTH + ATT_WIDTH:POOL_WIDTH + 2 * ATT_WIDTH].astype(BF16)
        v_ref[0] = z[:, POOL_WIDTH + 2 * ATT_WIDTH:].astype(BF16)


def _inproj(x, g, w_in):
    b, s, d = x.shape
    tm = LEFT
    nt = s // tm
    prev = lambda bi, i: (bi, jnp.maximum(i - 1, 0), 0)
    return pl.pallas_call(
        _inproj_kernel,
        grid=(b, nt + 1),
        in_specs=[
            pl.BlockSpec((1, tm, d), prev),
            pl.BlockSpec((1, d), lambda bi, i: (0, 0)),
            pl.BlockSpec(w_in.shape, lambda bi, i: (0, 0)),
        ],
        out_specs=[
            pl.BlockSpec((1, tm, POOL_WIDTH), prev),
            pl.BlockSpec((1, tm, ATT_WIDTH), prev),
            pl.BlockSpec((1, tm, ATT_WIDTH), lambda bi, i: (bi, i, 0)),
            pl.BlockSpec((1, tm, ATT_WIDTH), lambda bi, i: (bi, i, 0)),
        ],
        out_shape=[
            jax.ShapeDtypeStruct((b, s, POOL_WIDTH), F32),
            jax.ShapeDtypeStruct((b, s, ATT_WIDTH), BF16),
            jax.ShapeDtypeStruct((b, s + LEFT, ATT_WIDTH), BF16),
            jax.ShapeDtypeStruct((b, s + LEFT, ATT_WIDTH), BF16),
        ],
        compiler_params=pltpu.CompilerParams(
            dimension_semantics=("arbitrary", "arbitrary"), vmem_limit_bytes=VMEM_LIMIT),
        name="inproj",
    )(x, g, w_in)


def _mix_kernel(x_ref, p_ref, halo_ref, q_ref, k_ref, v_ref, bias_ref, wpool_ref, pscale_ref,
                wout_ref, o_ref, pbuf, mix, *, tq):
    i = pl.program_id(1)

    halo = halo_ref[0]
    pbuf[0:POOL_HALO, :] = jnp.where(i == 0, jnp.zeros_like(halo), halo)
    pbuf[POOL_HALO:, :] = p_ref[0]
    t_abs = i * tq + lax.broadcasted_iota(jnp.int32, (tq, 1), 0)
    for gi, w in enumerate(POOL_WINDOWS):
        cols = slice(gi * POOL_GROUP_DIM, (gi + 1) * POOL_GROUP_DIM)
        cur = pbuf[POOL_HALO:, cols]
        acc = cur
        for j in range(1, w):
            acc = acc + pbuf[POOL_HALO - j:POOL_HALO - j + tq, cols]
        cnt = jnp.minimum(t_abs + 1, w).astype(F32)
        pooled = (acc / cnt - cur).astype(BF16)
        y = jnp.dot(pooled, wpool_ref[gi], preferred_element_type=F32) * pscale_ref[:, cols]
        mix[:, cols] = y.astype(BF16)

    key_pos = lax.broadcasted_iota(jnp.int32, (1, BAND), 1) - LEFT
    for c in range(tq // CHUNK):
        chunk0 = pl.multiple_of((i * (tq // CHUNK) + c) * CHUNK, CHUNK)
        valid = (chunk0 + key_pos) >= 0
        kb = k_ref[0, pl.ds(chunk0, BAND), :]
        vb = v_ref[0, pl.ds(chunk0, BAND), :]
        qc = q_ref[0, c * CHUNK:(c + 1) * CHUNK, :]
        for h in range(ATT_HEADS):
            hc = slice(h * ATT_HEAD_DIM, (h + 1) * ATT_HEAD_DIM)
            s = lax.dot_general(qc[:, hc], kb[:, hc], (((1,), (1,)), ((), ())),
                                preferred_element_type=F32)
            s = jnp.where(valid, s + bias_ref[h], -1e30)
            m = jnp.max(s, axis=-1, keepdims=True)
            e = jnp.exp(s - m)
            pr = e / jnp.sum(e, axis=-1, keepdims=True)
            o = jnp.dot(pr.astype(BF16), vb[:, hc], preferred_element_type=F32)
            mix[c * CHUNK:(c + 1) * CHUNK, POOL_WIDTH + h * ATT_HEAD_DIM:
                POOL_WIDTH + (h + 1) * ATT_HEAD_DIM] = o.astype(BF16)

    o_ref[0] = x_ref[0] + jnp.dot(mix[...], wout_ref[...], preferred_element_type=F32)


def _mix(x, p, q, k, v, bias, w_pool, pool_scale, w_out):
    b, s, d = x.shape
    tq = 256
    halo_blocks = tq // POOL_HALO
    tile = lambda bi, i: (bi, i, 0)
    full = lambda bi, i: (bi, 0, 0)
    const2 = lambda bi, i: (0, 0)
    const3 = lambda bi, i: (0, 0, 0)
    return pl.pallas_call(
        functools.partial(_mix_kernel, tq=tq),
        grid=(b, s // tq),
        in_specs=[
            pl.BlockSpec((1, tq, d), tile),
            pl.BlockSpec((1, tq, POOL_WIDTH), tile),
            pl.BlockSpec((1, POOL_HALO, POOL_WIDTH),
                         lambda bi, i: (bi, jnp.maximum(i * halo_blocks - 1, 0), 0)),
            pl.BlockSpec((1, tq, ATT_WIDTH), tile),
            pl.BlockSpec((1, s + LEFT, ATT_WIDTH), full),
            pl.BlockSpec((1, s + LEFT, ATT_WIDTH), full),
            pl.BlockSpec(bias.shape, const3),
            pl.BlockSpec(w_pool.shape, const3),
            pl.BlockSpec(pool_scale.shape, const2),
            pl.BlockSpec(w_out.shape, const2),
        ],
        out_specs=pl.BlockSpec((1, tq, d), tile),
        out_shape=jax.ShapeDtypeStruct((b, s, d), F32),
        scratch_shapes=[
            pltpu.VMEM((POOL_HALO + tq, POOL_WIDTH), F32),
            pltpu.VMEM((tq, POOL_WIDTH + ATT_WIDTH), BF16),
        ],
        compiler_params=pltpu.CompilerParams(
            dimension_semantics=("arbitrary", "arbitrary"), vmem_limit_bytes=VMEM_LIMIT),
        name="pool_attn_mix",
    )(x, p, p, q, k, v, bias, w_pool, pool_scale, w_out)


def _sgu_kernel(h_ref, g_ref, win_ref, bin_ref, lng_ref, lnb_ref, ws_ref, bs_ref, wout_ref,
                o_ref, gated, *, tm):
    h = h_ref[...]
    d = h.shape[-1]
    xn = _rms(h, g_ref[...]).astype(BF16)
    z = jnp.dot(xn, win_ref[...], preferred_element_type=F32) + bin_ref[...]
    z = 0.5 * z * (1.0 + lax.erf(z * (2.0 ** -0.5)))
    u = z[:, :d]
    v = z[:, d:]
    mu = jnp.mean(v, axis=-1, keepdims=True)
    vc = v - mu
    var = jnp.mean(vc * vc, axis=-1, keepdims=True)
    vn = (vc * lax.rsqrt(var + EPS) * lng_ref[...] + lnb_ref[...]).astype(BF16)
    row = lax.broadcasted_iota(jnp.int32, (SGU_CHUNK, SGU_CHUNK), 0)
    col = lax.broadcasted_iota(jnp.int32, (SGU_CHUNK, SGU_CHUNK), 1)
    hd = d // SGU_HEADS
    for hh in range(SGU_HEADS):
        ws = jnp.where(row >= col, ws_ref[hh], jnp.zeros((), BF16))
        bs = bs_ref[hh]
        for n in range(tm // SGU_CHUNK):
            rows = slice(n * SGU_CHUNK, (n + 1) * SGU_CHUNK)
            cols = slice(hh * hd, (hh + 1) * hd)
            mixed = jnp.dot(ws, vn[rows, cols], preferred_element_type=F32) + bs
            gated[rows, cols] = (u[rows, cols] * mixed).astype(BF16)
    o_ref[...] = h + jnp.dot(gated[...], wout_ref[...], preferred_element_type=F32)


def _sgu(h, g, w_in, b_in, ln_g, ln_b, w_s, b_s, w_out):
    n, d = h.shape
    tm = 256
    tile = lambda i: (i, 0)
    const2 = lambda i: (0, 0)
    const3 = lambda i: (0, 0, 0)
    return pl.pallas_call(
        functools.partial(_sgu_kernel, tm=tm),
        grid=(n // tm,),
        in_specs=[
            pl.BlockSpec((tm, d), tile),
            pl.BlockSpec((1, d), const2),
            pl.BlockSpec(w_in.shape, const2),
            pl.BlockSpec(b_in.shape, const2),
            pl.BlockSpec(ln_g.shape, const2),
            pl.BlockSpec(ln_b.shape, const2),
            pl.BlockSpec(w_s.shape, const3),
            pl.BlockSpec(b_s.shape, const3),
            pl.BlockSpec(w_out.shape, const2),
        ],
        out_specs=pl.BlockSpec((tm, d), tile),
        out_shape=jax.ShapeDtypeStruct((n, d), F32),
        scratch_shapes=[pltpu.VMEM((tm, d), BF16)],
        compiler_params=pltpu.CompilerParams(
            dimension_semantics=("arbitrary",), vmem_limit_bytes=VMEM_LIMIT),
        name="sgu",
    )(h, g, w_in, b_in, ln_g, ln_b, w_s, b_s, w_out)


def _route(logits):
    lane = lax.broadcasted_iota(jnp.int32, logits.shape, 1)
    big = jnp.int32(ROUTER_LANES)
    neg = jnp.float32(-jnp.inf)
    gl = jnp.where(lane < N_GROUPS, logits, neg)
    gmax = jnp.max(gl, axis=-1, keepdims=True)
    gidx = jnp.min(jnp.where(gl == gmax, lane, big), axis=-1, keepdims=True)
    g_w = 1.0 / jnp.sum(jnp.exp(gl - gmax), axis=-1, keepdims=True)
    lo = N_GROUPS + EXPERTS_PER_GROUP * gidx
    el = jnp.where((lane >= lo) & (lane < lo + EXPERTS_PER_GROUP), logits, neg)
    m1 = jnp.max(el, axis=-1, keepdims=True)
    i1 = jnp.min(jnp.where(el == m1, lane, big), axis=-1, keepdims=True)
    el2 = jnp.where(lane == i1, neg, el)
    m2 = jnp.max(el2, axis=-1, keepdims=True)
    i2 = jnp.min(jnp.where(el2 == m2, lane, big), axis=-1, keepdims=True)
    e2 = jnp.exp(m2 - m1)
    den = 1.0 + e2
    w1 = g_w / den
    w2 = g_w * e2 / den
    return jnp.where(lane == i1, w1, 0.0) + jnp.where(lane == i2, w2, 0.0)


def _moe_kernel(h_ref, g_ref, wr_hi_ref, wr_lo_ref, br_ref, wg_ref, wu_ref, wd_ref, gf_ref,
                o_ref, xn_ref, comb_ref, *, final_norm):
    e = pl.program_id(1)

    @pl.when(e == 0)
    def _():
        h = h_ref[...]
        xn = _rms(h, g_ref[...])
        x_hi = xn.astype(BF16)
        x_lo = (xn - x_hi.astype(F32)).astype(BF16)
        logits = (jnp.dot(x_hi, wr_hi_ref[...], preferred_element_type=F32)
                  + jnp.dot(x_lo, wr_hi_ref[...], preferred_element_type=F32)
                  + jnp.dot(x_hi, wr_lo_ref[...], preferred_element_type=F32)) + br_ref[...]
        comb_ref[...] = _route(logits)
        xn_ref[...] = x_hi
        o_ref[...] = h

    xn = xn_ref[...]
    lane = lax.broadcasted_iota(jnp.int32, comb_ref.shape, 1)
    c = jnp.sum(jnp.where(lane == e + N_GROUPS, comb_ref[...], 0.0), axis=-1, keepdims=True)
    hg = jnp.dot(xn, wg_ref[0], preferred_element_type=F32)
    hu = jnp.dot(xn, wu_ref[0], preferred_element_type=F32)
    act = (hg * (1.0 / (1.0 + jnp.exp(-hg))) * hu * c).astype(BF16)
    o_ref[...] += jnp.dot(act, wd_ref[0], preferred_element_type=F32)

    if final_norm:
        @pl.when(e == N_EXPERTS - 1)
        def _():
            o_ref[...] = _rms(o_ref[...], gf_ref[...])


def _moe(h, g, wr_hi, wr_lo, br, wg, wu, wd, gf, final_norm):
    n, d = h.shape
    tm = 1024
    tile = lambda i, e: (i, 0)
    const2 = lambda i, e: (0, 0)
    expert = lambda i, e: (e, 0, 0)
    return pl.pallas_call(
        functools.partial(_moe_kernel, final_norm=final_norm),
        grid=(n // tm, N_EXPERTS),
        in_specs=[
            pl.BlockSpec((tm, d), tile),
            pl.BlockSpec((1, d), const2),
            pl.BlockSpec(wr_hi.shape, const2),
            pl.BlockSpec(wr_lo.shape, const2),
            pl.BlockSpec(br.shape, const2),
            pl.BlockSpec((1, d, D_EXPERT), expert),
            pl.BlockSpec((1, d, D_EXPERT), expert),
            pl.BlockSpec((1, D_EXPERT, d), expert),
            pl.BlockSpec((1, d), const2),
        ],
        out_specs=pl.BlockSpec((tm, d), tile),
        out_shape=jax.ShapeDtypeStruct((n, d), F32),
        scratch_shapes=[
            pltpu.VMEM((tm, d), BF16),
            pltpu.VMEM((tm, ROUTER_LANES), F32),
        ],
        compiler_params=pltpu.CompilerParams(
            dimension_semantics=("arbitrary", "arbitrary"), vmem_limit_bytes=VMEM_LIMIT),
        name="hier_moe",
    )(h, g, wr_hi, wr_lo, br, wg, wu, wd, gf)


def _router_params(wg_router, bg_router, we_router, be_router):
    d = wg_router.shape[0]
    we = jnp.transpose(we_router, (1, 0, 2)).reshape(d, N_EXPERTS)
    w = jnp.concatenate([wg_router, we], axis=1)
    w = jnp.pad(w, ((0, 0), (0, ROUTER_LANES - w.shape[1])))
    bias = jnp.concatenate([bg_router, be_router.reshape(N_EXPERTS)])
    bias = jnp.pad(bias, (0, ROUTER_LANES - bias.shape[0])).reshape(1, ROUTER_LANES)
    w_hi = w.astype(BF16)
    w_lo = (w - w_hi.astype(F32)).astype(BF16)
    return w_hi, w_lo, bias.astype(F32)


def _band_bias(rel_bias):
    rel = jnp.arange(BAND)[None, :] - LEFT - jnp.arange(CHUNK)[:, None]
    return rel_bias.astype(F32)[:, jnp.clip(rel, -MAX_REL, MAX_REL) + MAX_REL]


def kernel(x, norm_mix_g, norm_ffn_g, final_norm_g, ab_w_in, pool_w, pool_scale, att_rel_bias,
           ab_w_out, sgu_w_in, sgu_b_in, sgu_ln_g, sgu_ln_b, sgu_w_s, sgu_b_s, sgu_w_out,
           moe_wg_router, moe_bg_router, moe_we_router, moe_be_router,
           moe_w_gate, moe_w_up, moe_w_down):
    b, s, d = x.shape
    row = lambda a: a.reshape(1, -1).astype(F32)

    def moe(h, layer, final_norm):
        wr_hi, wr_lo, br = _router_params(moe_wg_router[layer], moe_bg_router[layer],
                                          moe_we_router[layer], moe_be_router[layer])
        return _moe(h, row(norm_ffn_g[layer]), wr_hi, wr_lo, br,
                    moe_w_gate[layer].astype(BF16), moe_w_up[layer].astype(BF16),
                    moe_w_down[layer].astype(BF16), row(final_norm_g), final_norm)

    p, q, k, v = _inproj(x, row(norm_mix_g[0]), ab_w_in[0].astype(BF16))
    h = _mix(x, p, q, k, v, _band_bias(att_rel_bias[0]), pool_w[0].astype(BF16),
             row(pool_scale[0]), ab_w_out[0].astype(BF16))
    h = moe(h.reshape(b * s, d), 0, False)

    h = _sgu(h, row(norm_mix_g[1]), sgu_w_in[0].astype(BF16), row(sgu_b_in[0]),
             row(sgu_ln_g[0]), row(sgu_ln_b[0]), sgu_w_s[0].astype(BF16),
             sgu_b_s[0].astype(F32)[:, :, None], sgu_w_out[0].astype(BF16))
    h = moe(h, 1, True)
    return h.reshape(b, s, d)
```

```python
import functools

import jax
import jax.numpy as jnp
from jax import lax
from jax.experimental import pallas as pl
from jax.experimental.pallas import tpu as pltpu

EPS = 1e-6
CHUNK = 64
POOL_WINDOWS = (2, 4, 8, 16)
POOL_GROUP_DIM = 128
POOL_WIDTH = 512
POOL_HALO = 16
ATT_HEADS = 8
ATT_HEAD_DIM = 64
ATT_WIDTH = 512
LEFT_CHUNKS = 8
LEFT = LEFT_CHUNKS * CHUNK
BAND = LEFT + CHUNK
MAX_REL = 128
ATT_SCALE = ATT_HEAD_DIM ** -0.5
SGU_CHUNK = 128
SGU_HEADS = 8
N_GROUPS = 4
EXPERTS_PER_GROUP = 4
N_EXPERTS = 16
D_EXPERT = 256
ROUTER_LANES = 128

VMEM_LIMIT = 56 * 1024 * 1024

F32 = jnp.float32
BF16 = jnp.bfloat16


def _rms(x, g):
    return x * lax.rsqrt(jnp.mean(x * x, axis=-1, keepdims=True) + EPS) * g


def _inproj_kernel(x_ref, g_ref, w_ref, p_ref, q_ref, k_ref, v_ref):
    i = pl.program_id(1)

    @pl.when(i == 0)
    def _():
        k_ref[...] = jnp.zeros_like(k_ref)
        v_ref[...] = jnp.zeros_like(v_ref)

    @pl.when(i > 0)
    def _():
        xn = _rms(x_ref[0], g_ref[...]).astype(BF16)
        z = jnp.dot(xn, w_ref[...], preferred_element_type=F32)
        p_ref[0] = z[:, :POOL_WIDTH]
        q_ref[0] = (z[:, POOL_WIDTH:POOL_WIDTH + ATT_WIDTH] * ATT_SCALE).astype(BF16)
        k_ref[0] = z[:, POOL_WIDTH + ATT_WIDTH:POOL_WIDTH + 2 * ATT_WIDTH].astype(BF16)
        v_ref[0] = z[:, POOL_WIDTH + 2 * ATT_WIDTH:].astype(BF16)


def _inproj(x, g, w_in):
    b, s, d = x.shape
    tm = LEFT
    nt = s // tm
    prev = lambda bi, i: (bi, jnp.maximum(i - 1, 0), 0)
    return pl.pallas_call(
        _inproj_kernel,
        grid=(b, nt + 1),
        in_specs=[
            pl.BlockSpec((1, tm, d), prev),
            pl.BlockSpec((1, d), lambda bi, i: (0, 0)),
            pl.BlockSpec(w_in.shape, lambda bi, i: (0, 0)),
        ],
        out_specs=[
            pl.BlockSpec((1, tm, POOL_WIDTH), prev),
            pl.BlockSpec((1, tm, ATT_WIDTH), prev),
            pl.BlockSpec((1, tm, ATT_WIDTH), lambda bi, i: (bi, i, 0)),
            pl.BlockSpec((1, tm, ATT_WIDTH), lambda bi, i: (bi, i, 0)),
        ],
        out_shape=[
            jax.ShapeDtypeStruct((b, s, POOL_WIDTH), F32),
            jax.ShapeDtypeStruct((b, s, ATT_WIDTH), BF16),
            jax.ShapeDtypeStruct((b, s + LEFT, ATT_WIDTH), BF16),
            jax.ShapeDtypeStruct((b, s + LEFT, ATT_WIDTH), BF16),
        ],
        compiler_params=pltpu.CompilerParams(
            dimension_semantics=("arbitrary", "arbitrary"), vmem_limit_bytes=VMEM_LIMIT),
        name="inproj",
    )(x, g, w_in)


def _attend_chunk(q_ref, k_ref, v_ref, bias_ref, mix, c, chunk0, masked):
    lane = lax.broadcasted_iota(jnp.int32, (CHUNK, 2 * ATT_HEAD_DIM), 1)
    even = lane < ATT_HEAD_DIM
    zero = jnp.zeros((), BF16)
    if masked:
        key_pos = lax.broadcasted_iota(jnp.int32, (1, BAND), 1) - LEFT
        valid = (chunk0 + key_pos) >= 0
    for j in range(ATT_HEADS // 2):
        cols = slice(2 * j * ATT_HEAD_DIM, 2 * (j + 1) * ATT_HEAD_DIM)
        q2 = q_ref[0, c * CHUNK:(c + 1) * CHUNK, cols]
        qs = jnp.concatenate([jnp.where(even, q2, zero), jnp.where(even, zero, q2)], axis=0)
        kb = k_ref[0, pl.ds(chunk0, BAND), cols]
        vb = v_ref[0, pl.ds(chunk0, BAND), cols]
        s = lax.dot_general(qs, kb, (((1,), (1,)), ((), ())), preferred_element_type=F32)
        s = s + bias_ref[j]
        if masked:
            s = jnp.where(valid, s, -1e30)
        m = jnp.max(s, axis=-1, keepdims=True)
        e = jnp.exp(s - m)
        r = 1.0 / jnp.sum(e, axis=-1, keepdims=True)
        o = jnp.dot(e.astype(BF16), vb, preferred_element_type=F32) * r
        o2 = jnp.where(even, o[:CHUNK], o[CHUNK:])
        mix[c * CHUNK:(c + 1) * CHUNK, POOL_WIDTH + 2 * j * ATT_HEAD_DIM:
            POOL_WIDTH + 2 * (j + 1) * ATT_HEAD_DIM] = o2.astype(BF16)


def _mix_kernel(x_ref, p_ref, halo_ref, q_ref, k_ref, v_ref, bias_ref, wpool_ref, pscale_ref,
                wout_ref, o_ref, pbuf, mix, *, tq):
    i = pl.program_id(1)

    halo = halo_ref[0]
    pbuf[0:POOL_HALO, :] = jnp.where(i == 0, jnp.zeros_like(halo), halo)
    pbuf[POOL_HALO:, :] = p_ref[0]
    t_abs = i * tq + lax.broadcasted_iota(jnp.int32, (tq, 1), 0)
    for gi, w in enumerate(POOL_WINDOWS):
        cols = slice(gi * POOL_GROUP_DIM, (gi + 1) * POOL_GROUP_DIM)
        cur = pbuf[POOL_HALO:, cols]
        acc = cur
        for j in range(1, w):
            acc = acc + pbuf[POOL_HALO - j:POOL_HALO - j + tq, cols]
        cnt = jnp.minimum(t_abs + 1, w).astype(F32)
        pooled = (acc / cnt - cur).astype(BF16)
        y = jnp.dot(pooled, wpool_ref[gi], preferred_element_type=F32) * pscale_ref[:, cols]
        mix[:, cols] = y.astype(BF16)

    def attend(masked):
        for c in range(tq // CHUNK):
            chunk0 = pl.multiple_of((i * (tq // CHUNK) + c) * CHUNK, CHUNK)
            _attend_chunk(q_ref, k_ref, v_ref, bias_ref, mix, c, chunk0, masked)

    pl.when(i * tq < LEFT)(lambda: attend(True))
    pl.when(i * tq >= LEFT)(lambda: attend(False))

    o_ref[0] = x_ref[0] + jnp.dot(mix[...], wout_ref[...], preferred_element_type=F32)


def _mix(x, p, q, k, v, bias, w_pool, pool_scale, w_out):
    b, s, d = x.shape
    tq = 256
    halo_blocks = tq // POOL_HALO
    tile = lambda bi, i: (bi, i, 0)
    full = lambda bi, i: (bi, 0, 0)
    const2 = lambda bi, i: (0, 0)
    const3 = lambda bi, i: (0, 0, 0)
    return pl.pallas_call(
        functools.partial(_mix_kernel, tq=tq),
        grid=(b, s // tq),
        in_specs=[
            pl.BlockSpec((1, tq, d), tile),
            pl.BlockSpec((1, tq, POOL_WIDTH), tile),
            pl.BlockSpec((1, POOL_HALO, POOL_WIDTH),
                         lambda bi, i: (bi, jnp.maximum(i * halo_blocks - 1, 0), 0)),
            pl.BlockSpec((1, tq, ATT_WIDTH), tile),
            pl.BlockSpec((1, s + LEFT, ATT_WIDTH), full),
            pl.BlockSpec((1, s + LEFT, ATT_WIDTH), full),
            pl.BlockSpec(bias.shape, const3),
            pl.BlockSpec(w_pool.shape, const3),
            pl.BlockSpec(pool_scale.shape, const2),
            pl.BlockSpec(w_out.shape, const2),
        ],
        out_specs=pl.BlockSpec((1, tq, d), tile),
        out_shape=jax.ShapeDtypeStruct((b, s, d), F32),
        scratch_shapes=[
            pltpu.VMEM((POOL_HALO + tq, POOL_WIDTH), F32),
            pltpu.VMEM((tq, POOL_WIDTH + ATT_WIDTH), BF16),
        ],
        compiler_params=pltpu.CompilerParams(
            dimension_semantics=("arbitrary", "arbitrary"), vmem_limit_bytes=VMEM_LIMIT),
        name="pool_attn_mix",
    )(x, p, p, q, k, v, bias, w_pool, pool_scale, w_out)


def _sgu_kernel(h_ref, g_ref, win_ref, bin_ref, lng_ref, lnb_ref, ws_ref, bs_ref, wout_ref,
                o_ref, gated, *, tm):
    h = h_ref[...]
    d = h.shape[-1]
    xn = _rms(h, g_ref[...]).astype(BF16)
    z = jnp.dot(xn, win_ref[...], preferred_element_type=F32) + bin_ref[...]
    z = 0.5 * z * (1.0 + lax.erf(z * (2.0 ** -0.5)))
    u = z[:, :d]
    v = z[:, d:]
    mu = jnp.mean(v, axis=-1, keepdims=True)
    vc = v - mu
    var = jnp.mean(vc * vc, axis=-1, keepdims=True)
    vn = (vc * lax.rsqrt(var + EPS) * lng_ref[...] + lnb_ref[...]).astype(BF16)
    row = lax.broadcasted_iota(jnp.int32, (SGU_CHUNK, SGU_CHUNK), 0)
    col = lax.broadcasted_iota(jnp.int32, (SGU_CHUNK, SGU_CHUNK), 1)
    hd = d // SGU_HEADS
    for hh in range(SGU_HEADS):
        ws = jnp.where(row >= col, ws_ref[hh], jnp.zeros((), BF16))
        bs = bs_ref[hh]
        for n in range(tm // SGU_CHUNK):
            rows = slice(n * SGU_CHUNK, (n + 1) * SGU_CHUNK)
            cols = slice(hh * hd, (hh + 1) * hd)
            mixed = jnp.dot(ws, vn[rows, cols], preferred_element_type=F32) + bs
            gated[rows, cols] = (u[rows, cols] * mixed).astype(BF16)
    o_ref[...] = h + jnp.dot(gated[...], wout_ref[...], preferred_element_type=F32)


def _sgu(h, g, w_in, b_in, ln_g, ln_b, w_s, b_s, w_out):
    n, d = h.shape
    tm = 256
    tile = lambda i: (i, 0)
    const2 = lambda i: (0, 0)
    const3 = lambda i: (0, 0, 0)
    return pl.pallas_call(
        functools.partial(_sgu_kernel, tm=tm),
        grid=(n // tm,),
        in_specs=[
            pl.BlockSpec((tm, d), tile),
            pl.BlockSpec((1, d), const2),
            pl.BlockSpec(w_in.shape, const2),
            pl.BlockSpec(b_in.shape, const2),
            pl.BlockSpec(ln_g.shape, const2),
            pl.BlockSpec(ln_b.shape, const2),
            pl.BlockSpec(w_s.shape, const3),
            pl.BlockSpec(b_s.shape, const3),
            pl.BlockSpec(w_out.shape, const2),
        ],
        out_specs=pl.BlockSpec((tm, d), tile),
        out_shape=jax.ShapeDtypeStruct((n, d), F32),
        scratch_shapes=[pltpu.VMEM((tm, d), BF16)],
        compiler_params=pltpu.CompilerParams(
            dimension_semantics=("arbitrary",), vmem_limit_bytes=VMEM_LIMIT),
        name="sgu",
    )(h, g, w_in, b_in, ln_g, ln_b, w_s, b_s, w_out)


def _route(logits):
    lane = lax.broadcasted_iota(jnp.int32, logits.shape, 1)
    big = jnp.int32(ROUTER_LANES)
    neg = jnp.float32(-jnp.inf)
    gl = jnp.where(lane < N_GROUPS, logits, neg)
    gmax = jnp.max(gl, axis=-1, keepdims=True)
    gidx = jnp.min(jnp.where(gl == gmax, lane, big), axis=-1, keepdims=True)
    g_w = 1.0 / jnp.sum(jnp.exp(gl - gmax), axis=-1, keepdims=True)
    lo = N_GROUPS + EXPERTS_PER_GROUP * gidx
    el = jnp.where((lane >= lo) & (lane < lo + EXPERTS_PER_GROUP), logits, neg)
    m1 = jnp.max(el, axis=-1, keepdims=True)
    i1 = jnp.min(jnp.where(el == m1, lane, big), axis=-1, keepdims=True)
    el2 = jnp.where(lane == i1, neg, el)
    m2 = jnp.max(el2, axis=-1, keepdims=True)
    i2 = jnp.min(jnp.where(el2 == m2, lane, big), axis=-1, keepdims=True)
    e2 = jnp.exp(m2 - m1)
    den = 1.0 + e2
    w1 = g_w / den
    w2 = g_w * e2 / den
    return jnp.where(lane == i1, w1, 0.0) + jnp.where(lane == i2, w2, 0.0)


def _moe_kernel(h_ref, g_ref, wr_hi_ref, wr_lo_ref, br_ref, wg_ref, wu_ref, wd_ref, gf_ref,
                o_ref, xn_ref, comb_ref, *, final_norm):
    e = pl.program_id(1)

    @pl.when(e == 0)
    def _():
        h = h_ref[...]
        xn = _rms(h, g_ref[...])
        x_hi = xn.astype(BF16)
        x_lo = (xn - x_hi.astype(F32)).astype(BF16)
        logits = (jnp.dot(x_hi, wr_hi_ref[...], preferred_element_type=F32)
                  + jnp.dot(x_lo, wr_hi_ref[...], preferred_element_type=F32)
                  + jnp.dot(x_hi, wr_lo_ref[...], preferred_element_type=F32)) + br_ref[...]
        comb_ref[...] = _route(logits)
        xn_ref[...] = x_hi
        o_ref[...] = h

    xn = xn_ref[...]
    lane = lax.broadcasted_iota(jnp.int32, comb_ref.shape, 1)
    c = jnp.sum(jnp.where(lane == e + N_GROUPS, comb_ref[...], 0.0), axis=-1, keepdims=True)
    hg = jnp.dot(xn, wg_ref[0], preferred_element_type=F32)
    hu = jnp.dot(xn, wu_ref[0], preferred_element_type=F32)
    act = (hg * (1.0 / (1.0 + jnp.exp(-hg))) * hu * c).astype(BF16)
    o_ref[...] += jnp.dot(act, wd_ref[0], preferred_element_type=F32)

    if final_norm:
        @pl.when(e == N_EXPERTS - 1)
        def _():
            o_ref[...] = _rms(o_ref[...], gf_ref[...])


def _moe(h, g, wr_hi, wr_lo, br, wg, wu, wd, gf, final_norm):
    n, d = h.shape
    tm = 1024
    tile = lambda i, e: (i, 0)
    const2 = lambda i, e: (0, 0)
    expert = lambda i, e: (e, 0, 0)
    return pl.pallas_call(
        functools.partial(_moe_kernel, final_norm=final_norm),
        grid=(n // tm, N_EXPERTS),
        in_specs=[
            pl.BlockSpec((tm, d), tile),
            pl.BlockSpec((1, d), const2),
            pl.BlockSpec(wr_hi.shape, const2),
            pl.BlockSpec(wr_lo.shape, const2),
            pl.BlockSpec(br.shape, const2),
            pl.BlockSpec((1, d, D_EXPERT), expert),
            pl.BlockSpec((1, d, D_EXPERT), expert),
            pl.BlockSpec((1, D_EXPERT, d), expert),
            pl.BlockSpec((1, d), const2),
        ],
        out_specs=pl.BlockSpec((tm, d), tile),
        out_shape=jax.ShapeDtypeStruct((n, d), F32),
        scratch_shapes=[
            pltpu.VMEM((tm, d), BF16),
            pltpu.VMEM((tm, ROUTER_LANES), F32),
        ],
        compiler_params=pltpu.CompilerParams(
            dimension_semantics=("arbitrary", "arbitrary"), vmem_limit_bytes=VMEM_LIMIT),
        name="hier_moe",
    )(h, g, wr_hi, wr_lo, br, wg, wu, wd, gf)


def _router_params(wg_router, bg_router, we_router, be_router):
    d = wg_router.shape[0]
    we = jnp.transpose(we_router, (1, 0, 2)).reshape(d, N_EXPERTS)
    w = jnp.concatenate([wg_router, we], axis=1)
    w = jnp.pad(w, ((0, 0), (0, ROUTER_LANES - w.shape[1])))
    bias = jnp.concatenate([bg_router, be_router.reshape(N_EXPERTS)])
    bias = jnp.pad(bias, (0, ROUTER_LANES - bias.shape[0])).reshape(1, ROUTER_LANES)
    w_hi = w.astype(BF16)
    w_lo = (w - w_hi.astype(F32)).astype(BF16)
    return w_hi, w_lo, bias.astype(F32)


def _band_bias(rel_bias):
    m = jnp.arange(BAND + CHUNK - 1) - (CHUNK - 1) - LEFT
    ext = rel_bias.astype(F32)[:, jnp.clip(m, -MAX_REL, MAX_REL) + MAX_REL]
    bias = jnp.stack([ext[:, CHUNK - 1 - i:CHUNK - 1 - i + BAND] for i in range(CHUNK)], axis=1)
    return bias.reshape(ATT_HEADS // 2, 2 * CHUNK, BAND)


def kernel(x, norm_mix_g, norm_ffn_g, final_norm_g, ab_w_in, pool_w, pool_scale, att_rel_bias,
           ab_w_out, sgu_w_in, sgu_b_in, sgu_ln_g, sgu_ln_b, sgu_w_s, sgu_b_s, sgu_w_out,
           moe_wg_router, moe_bg_router, moe_we_router, moe_be_router,
           moe_w_gate, moe_w_up, moe_w_down):
    b, s, d = x.shape
    row = lambda a: a.reshape(1, -1).astype(F32)

    def moe(h, layer, final_norm):
        wr_hi, wr_lo, br = _router_params(moe_wg_router[layer], moe_bg_router[layer],
                                          moe_we_router[layer], moe_be_router[layer])
        return _moe(h, row(norm_ffn_g[layer]), wr_hi, wr_lo, br,
                    moe_w_gate[layer].astype(BF16), moe_w_up[layer].astype(BF16),
                    moe_w_down[layer].astype(BF16), row(final_norm_g), final_norm)

    p, q, k, v = _inproj(x, row(norm_mix_g[0]), ab_w_in[0].astype(BF16))
    h = _mix(x, p, q, k, v, _band_bias(att_rel_bias[0]), pool_w[0].astype(BF16),
             row(pool_scale[0]), ab_w_out[0].astype(BF16))
    h = moe(h.reshape(b * s, d), 0, False)

    h = _sgu(h, row(norm_mix_g[1]), sgu_w_in[0].astype(BF16), row(sgu_b_in[0]),
             row(sgu_ln_g[0]), row(sgu_ln_b[0]), sgu_w_s[0].astype(BF16),
             sgu_b_s[0].astype(F32)[:, :, None], sgu_w_out[0].astype(BF16))
    h = moe(h, 1, True)
    return h.reshape(b, s, d)
```

```python
import functools

import jax
import jax.numpy as jnp
from jax import lax
from jax.experimental import pallas as pl
from jax.experimental.pallas import tpu as pltpu

EPS = 1e-6
CHUNK = 64
POOL_WINDOWS = (2, 4, 8, 16)
POOL_GROUP_DIM = 128
POOL_WIDTH = 512
POOL_HALO = 16
ATT_HEADS = 8
ATT_HEAD_DIM = 64
ATT_WIDTH = 512
LEFT_CHUNKS = 8
LEFT = LEFT_CHUNKS * CHUNK
BAND = LEFT + CHUNK
PAIR = 2 * CHUNK
PAIR_BAND = LEFT + PAIR
MAX_REL = 128
ATT_SCALE = ATT_HEAD_DIM ** -0.5
LOG2E = 1.4426950408889634
VT_ONES = 16
VT_PAIR = 2 * ATT_HEAD_DIM + VT_ONES
VT_ROWS = (ATT_HEADS // 2) * VT_PAIR
SGU_CHUNK = 128
SGU_HEADS = 8
N_GROUPS = 4
EXPERTS_PER_GROUP = 4
N_EXPERTS = 16
D_EXPERT = 256
ROUTER_LANES = 128

VMEM_LIMIT = 56 * 1024 * 1024

F32 = jnp.float32
BF16 = jnp.bfloat16


def _rms(x, g):
    return x * lax.rsqrt(jnp.mean(x * x, axis=-1, keepdims=True) + EPS) * g


def _inproj_kernel(x_ref, g_ref, w_ref, wvt_ref, p_ref, q_ref, k_ref, vt_ref):
    i = pl.program_id(1)

    @pl.when(i == 0)
    def _():
        k_ref[...] = jnp.zeros_like(k_ref)
        vt_ref[...] = jnp.zeros_like(vt_ref)

    @pl.when(i > 0)
    def _():
        xn = _rms(x_ref[0], g_ref[...]).astype(BF16)
        z = jnp.dot(xn, w_ref[...], preferred_element_type=F32)
        p_ref[0] = z[:, :POOL_WIDTH]
        q_ref[0] = (z[:, POOL_WIDTH:POOL_WIDTH + ATT_WIDTH] * (ATT_SCALE * LOG2E)).astype(BF16)
        k_ref[0] = z[:, POOL_WIDTH + ATT_WIDTH:].astype(BF16)
        vt = lax.dot_general(wvt_ref[...], xn, (((1,), (1,)), ((), ())), preferred_element_type=F32)
        for j in range(ATT_HEADS // 2):
            rows = slice(2 * j * ATT_HEAD_DIM, 2 * (j + 1) * ATT_HEAD_DIM)
            vt_ref[0, j * VT_PAIR:j * VT_PAIR + 2 * ATT_HEAD_DIM, :] = vt[rows].astype(BF16)
            vt_ref[0, j * VT_PAIR + 2 * ATT_HEAD_DIM:(j + 1) * VT_PAIR, :] = jnp.ones(
                (VT_ONES, vt.shape[1]), BF16)


def _inproj(x, g, w_pqk, w_vt):
    b, s, d = x.shape
    tm = LEFT
    nt = s // tm
    prev = lambda bi, i: (bi, jnp.maximum(i - 1, 0), 0)
    const2 = lambda bi, i: (0, 0)
    return pl.pallas_call(
        _inproj_kernel,
        grid=(b, nt + 1),
        in_specs=[
            pl.BlockSpec((1, tm, d), prev),
            pl.BlockSpec((1, d), const2),
            pl.BlockSpec(w_pqk.shape, const2),
            pl.BlockSpec(w_vt.shape, const2),
        ],
        out_specs=[
            pl.BlockSpec((1, tm, POOL_WIDTH), prev),
            pl.BlockSpec((1, tm, ATT_WIDTH), prev),
            pl.BlockSpec((1, tm, ATT_WIDTH), lambda bi, i: (bi, i, 0)),
            pl.BlockSpec((1, VT_ROWS, tm), lambda bi, i: (bi, 0, i)),
        ],
        out_shape=[
            jax.ShapeDtypeStruct((b, s, POOL_WIDTH), F32),
            jax.ShapeDtypeStruct((b, s, ATT_WIDTH), BF16),
            jax.ShapeDtypeStruct((b, s + LEFT, ATT_WIDTH), BF16),
            jax.ShapeDtypeStruct((b, VT_ROWS, s + LEFT), BF16),
        ],
        compiler_params=pltpu.CompilerParams(
            dimension_semantics=("arbitrary", "arbitrary"), vmem_limit_bytes=VMEM_LIMIT),
        name="inproj",
    )(x, g, w_pqk, w_vt)


def _attend_tile(q_ref, k_ref, vt_ref, biast_ref, mix, st_buf, i, tq, masked):
    lane = lax.broadcasted_iota(jnp.int32, (PAIR, 2 * ATT_HEAD_DIM), 1)
    even = lane < ATT_HEAD_DIM
    zero = jnp.zeros((), BF16)
    units = [(cp, j) for cp in range(tq // PAIR) for j in range(ATT_HEADS // 2)]

    def band_base(cp):
        return pl.multiple_of((i * (tq // PAIR) + cp) * PAIR, PAIR)

    def scores(u):
        cp, j = units[u]
        cols = slice(2 * j * ATT_HEAD_DIM, 2 * (j + 1) * ATT_HEAD_DIM)
        q2 = q_ref[0, cp * PAIR:(cp + 1) * PAIR, cols]
        qs = jnp.concatenate([jnp.where(even, q2, zero), jnp.where(even, zero, q2)], axis=0)
        kb = k_ref[0, pl.ds(band_base(cp), PAIR_BAND), cols]
        st_buf[u % 2] = lax.dot_general(kb, qs, (((1,), (1,)), ((), ())),
                                        preferred_element_type=F32)

    scores(0)
    for u, (cp, j) in enumerate(units):
        if u + 1 < len(units):
            scores(u + 1)
        cols = slice(2 * j * ATT_HEAD_DIM, 2 * (j + 1) * ATT_HEAD_DIM)
        base = band_base(cp)
        st = st_buf[u % 2] + biast_ref[j]
        if masked:
            valid = (lax.broadcasted_iota(jnp.int32, (PAIR_BAND, 1), 0) + base - LEFT) >= 0
            st = jnp.where(valid, st, -1e30)
        m = jnp.max(st, axis=0, keepdims=True)
        e = jnp.exp2(st - m).astype(BF16)
        vtb = vt_ref[0, j * VT_PAIR:(j + 1) * VT_PAIR, pl.ds(base, PAIR_BAND)]
        ot = jnp.dot(vtb, e, preferred_element_type=F32)
        ot = ot[:2 * ATT_HEAD_DIM] * (1.0 / ot[2 * ATT_HEAD_DIM:2 * ATT_HEAD_DIM + 1])
        o2t = jnp.concatenate([ot[:ATT_HEAD_DIM, :PAIR], ot[ATT_HEAD_DIM:, PAIR:]], axis=0)
        mix[cp * PAIR:(cp + 1) * PAIR, POOL_WIDTH + 2 * j * ATT_HEAD_DIM:
            POOL_WIDTH + 2 * (j + 1) * ATT_HEAD_DIM] = o2t.T.astype(BF16)


def _mix_kernel(x_ref, p_ref, halo_ref, q_ref, k_ref, vt_ref, biast_ref, wpool_ref, pscale_ref,
                wout_ref, o_ref, pbuf, mix, st_buf, *, tq):
    i = pl.program_id(1)

    halo = halo_ref[0]
    pbuf[0:POOL_HALO, :] = jnp.where(i == 0, jnp.zeros_like(halo), halo)
    pbuf[POOL_HALO:, :] = p_ref[0]
    t_abs = i * tq + lax.broadcasted_iota(jnp.int32, (tq, 1), 0)
    for gi, w in enumerate(POOL_WINDOWS):
        cols = slice(gi * POOL_GROUP_DIM, (gi + 1) * POOL_GROUP_DIM)
        cur = pbuf[POOL_HALO:, cols]
        acc = cur
        for j in range(1, w):
            acc = acc + pbuf[POOL_HALO - j:POOL_HALO - j + tq, cols]
        cnt = jnp.minimum(t_abs + 1, w).astype(F32)
        pooled = (acc / cnt - cur).astype(BF16)
        y = jnp.dot(pooled, wpool_ref[gi], preferred_element_type=F32) * pscale_ref[:, cols]
        mix[:, cols] = y.astype(BF16)

    def attend(masked):
        _attend_tile(q_ref, k_ref, vt_ref, biast_ref, mix, st_buf, i, tq, masked)

    pl.when(i * tq < LEFT)(lambda: attend(True))
    pl.when(i * tq >= LEFT)(lambda: attend(False))

    o_ref[0] = x_ref[0] + jnp.dot(mix[...], wout_ref[...], preferred_element_type=F32)


def _mix(x, p, q, k, vt, biast, w_pool, pool_scale, w_out):
    b, s, d = x.shape
    tq = 256
    halo_blocks = tq // POOL_HALO
    tile = lambda bi, i: (bi, i, 0)
    full = lambda bi, i: (bi, 0, 0)
    const2 = lambda bi, i: (0, 0)
    const3 = lambda bi, i: (0, 0, 0)
    return pl.pallas_call(
        functools.partial(_mix_kernel, tq=tq),
        grid=(b, s // tq),
        in_specs=[
            pl.BlockSpec((1, tq, d), tile),
            pl.BlockSpec((1, tq, POOL_WIDTH), tile),
            pl.BlockSpec((1, POOL_HALO, POOL_WIDTH),
                         lambda bi, i: (bi, jnp.maximum(i * halo_blocks - 1, 0), 0)),
            pl.BlockSpec((1, tq, ATT_WIDTH), tile),
            pl.BlockSpec((1, s + LEFT, ATT_WIDTH), full),
            pl.BlockSpec((1, VT_ROWS, s + LEFT), full),
            pl.BlockSpec(biast.shape, const3),
            pl.BlockSpec(w_pool.shape, const3),
            pl.BlockSpec(pool_scale.shape, const2),
            pl.BlockSpec(w_out.shape, const2),
        ],
        out_specs=pl.BlockSpec((1, tq, d), tile),
        out_shape=jax.ShapeDtypeStruct((b, s, d), F32),
        scratch_shapes=[
            pltpu.VMEM((POOL_HALO + tq, POOL_WIDTH), F32),
            pltpu.VMEM((tq, POOL_WIDTH + ATT_WIDTH), BF16),
            pltpu.VMEM((2, PAIR_BAND, 2 * PAIR), F32),
        ],
        compiler_params=pltpu.CompilerParams(
            dimension_semantics=("arbitrary", "arbitrary"), vmem_limit_bytes=VMEM_LIMIT),
        name="pool_attn_mix",
    )(x, p, p, q, k, vt, biast, w_pool, pool_scale, w_out)


def _sgu_kernel(h_ref, g_ref, win_ref, bin_ref, lng_ref, lnb_ref, ws_ref, bs_ref, wout_ref,
                o_ref, gated, *, tm):
    h = h_ref[...]
    d = h.shape[-1]
    xn = _rms(h, g_ref[...]).astype(BF16)
    z = jnp.dot(xn, win_ref[...], preferred_element_type=F32) + bin_ref[...]
    z = 0.5 * z * (1.0 + lax.erf(z * (2.0 ** -0.5)))
    u = z[:, :d]
    v = z[:, d:]
    mu = jnp.mean(v, axis=-1, keepdims=True)
    vc = v - mu
    var = jnp.mean(vc * vc, axis=-1, keepdims=True)
    vn = (vc * lax.rsqrt(var + EPS) * lng_ref[...] + lnb_ref[...]).astype(BF16)
    row = lax.broadcasted_iota(jnp.int32, (SGU_CHUNK, SGU_CHUNK), 0)
    col = lax.broadcasted_iota(jnp.int32, (SGU_CHUNK, SGU_CHUNK), 1)
    hd = d // SGU_HEADS
    for hh in range(SGU_HEADS):
        ws = jnp.where(row >= col, ws_ref[hh], jnp.zeros((), BF16))
        bs = bs_ref[hh]
        for n in range(tm // SGU_CHUNK):
            rows = slice(n * SGU_CHUNK, (n + 1) * SGU_CHUNK)
            cols = slice(hh * hd, (hh + 1) * hd)
            mixed = jnp.dot(ws, vn[rows, cols], preferred_element_type=F32) + bs
            gated[rows, cols] = (u[rows, cols] * mixed).astype(BF16)
    o_ref[...] = h + jnp.dot(gated[...], wout_ref[...], preferred_element_type=F32)


def _sgu(h, g, w_in, b_in, ln_g, ln_b, w_s, b_s, w_out):
    n, d = h.shape
    tm = 256
    tile = lambda i: (i, 0)
    const2 = lambda i: (0, 0)
    const3 = lambda i: (0, 0, 0)
    return pl.pallas_call(
        functools.partial(_sgu_kernel, tm=tm),
        grid=(n // tm,),
        in_specs=[
            pl.BlockSpec((tm, d), tile),
            pl.BlockSpec((1, d), const2),
            pl.BlockSpec(w_in.shape, const2),
            pl.BlockSpec(b_in.shape, const2),
            pl.BlockSpec(ln_g.shape, const2),
            pl.BlockSpec(ln_b.shape, const2),
            pl.BlockSpec(w_s.shape, const3),
            pl.BlockSpec(b_s.shape, const3),
            pl.BlockSpec(w_out.shape, const2),
        ],
        out_specs=pl.BlockSpec((tm, d), tile),
        out_shape=jax.ShapeDtypeStruct((n, d), F32),
        scratch_shapes=[pltpu.VMEM((tm, d), BF16)],
        compiler_params=pltpu.CompilerParams(
            dimension_semantics=("arbitrary",), vmem_limit_bytes=VMEM_LIMIT),
        name="sgu",
    )(h, g, w_in, b_in, ln_g, ln_b, w_s, b_s, w_out)


def _route(logits):
    lane = lax.broadcasted_iota(jnp.int32, logits.shape, 1)
    big = jnp.int32(ROUTER_LANES)
    neg = jnp.float32(-jnp.inf)
    gl = jnp.where(lane < N_GROUPS, logits, neg)
    gmax = jnp.max(gl, axis=-1, keepdims=True)
    gidx = jnp.min(jnp.where(gl == gmax, lane, big), axis=-1, keepdims=True)
    g_w = 1.0 / jnp.sum(jnp.exp(gl - gmax), axis=-1, keepdims=True)
    lo = N_GROUPS + EXPERTS_PER_GROUP * gidx
    el = jnp.where((lane >= lo) & (lane < lo + EXPERTS_PER_GROUP), logits, neg)
    m1 = jnp.max(el, axis=-1, keepdims=True)
    i1 = jnp.min(jnp.where(el == m1, lane, big), axis=-1, keepdims=True)
    el2 = jnp.where(lane == i1, neg, el)
    m2 = jnp.max(el2, axis=-1, keepdims=True)
    i2 = jnp.min(jnp.where(el2 == m2, lane, big), axis=-1, keepdims=True)
    e2 = jnp.exp(m2 - m1)
    den = 1.0 + e2
    w1 = g_w / den
    w2 = g_w * e2 / den
    return jnp.where(lane == i1, w1, 0.0) + jnp.where(lane == i2, w2, 0.0)


def _moe_kernel(h_ref, g_ref, wr_hi_ref, wr_lo_ref, br_ref, wg_ref, wu_ref, wd_ref, gf_ref,
                o_ref, xn_ref, comb_ref, *, final_norm):
    e = pl.program_id(1)

    @pl.when(e == 0)
    def _():
        h = h_ref[...]
        xn = _rms(h, g_ref[...])
        x_hi = xn.astype(BF16)
        x_lo = (xn - x_hi.astype(F32)).astype(BF16)
        logits = (jnp.dot(x_hi, wr_hi_ref[...], preferred_element_type=F32)
                  + jnp.dot(x_lo, wr_hi_ref[...], preferred_element_type=F32)
                  + jnp.dot(x_hi, wr_lo_ref[...], preferred_element_type=F32)) + br_ref[...]
        comb_ref[...] = _route(logits)
        xn_ref[...] = x_hi
        o_ref[...] = h

    xn = xn_ref[...]
    lane = lax.broadcasted_iota(jnp.int32, comb_ref.shape, 1)
    c = jnp.sum(jnp.where(lane == e + N_GROUPS, comb_ref[...], 0.0), axis=-1, keepdims=True)
    hg = jnp.dot(xn, wg_ref[0], preferred_element_type=F32)
    hu = jnp.dot(xn, wu_ref[0], preferred_element_type=F32)
    act = (hg * (1.0 / (1.0 + jnp.exp(-hg))) * hu * c).astype(BF16)
    o_ref[...] += jnp.dot(act, wd_ref[0], preferred_element_type=F32)

    if final_norm:
        @pl.when(e == N_EXPERTS - 1)
        def _():
            o_ref[...] = _rms(o_ref[...], gf_ref[...])


def _moe(h, g, wr_hi, wr_lo, br, wg, wu, wd, gf, final_norm):
    n, d = h.shape
    tm = 1024
    tile = lambda i, e: (i, 0)
    const2 = lambda i, e: (0, 0)
    expert = lambda i, e: (e, 0, 0)
    return pl.pallas_call(
        functools.partial(_moe_kernel, final_norm=final_norm),
        grid=(n // tm, N_EXPERTS),
        in_specs=[
            pl.BlockSpec((tm, d), tile),
            pl.BlockSpec((1, d), const2),
            pl.BlockSpec(wr_hi.shape, const2),
            pl.BlockSpec(wr_lo.shape, const2),
            pl.BlockSpec(br.shape, const2),
            pl.BlockSpec((1, d, D_EXPERT), expert),
            pl.BlockSpec((1, d, D_EXPERT), expert),
            pl.BlockSpec((1, D_EXPERT, d), expert),
            pl.BlockSpec((1, d), const2),
        ],
        out_specs=pl.BlockSpec((tm, d), tile),
        out_shape=jax.ShapeDtypeStruct((n, d), F32),
        scratch_shapes=[
            pltpu.VMEM((tm, d), BF16),
            pltpu.VMEM((tm, ROUTER_LANES), F32),
        ],
        compiler_params=pltpu.CompilerParams(
            dimension_semantics=("arbitrary", "arbitrary"), vmem_limit_bytes=VMEM_LIMIT),
        name="hier_moe",
    )(h, g, wr_hi, wr_lo, br, wg, wu, wd, gf)


def _router_params(wg_router, bg_router, we_router, be_router):
    d = wg_router.shape[0]
    we = jnp.transpose(we_router, (1, 0, 2)).reshape(d, N_EXPERTS)
    w = jnp.concatenate([wg_router, we], axis=1)
    w = jnp.pad(w, ((0, 0), (0, ROUTER_LANES - w.shape[1])))
    bias = jnp.concatenate([bg_router, be_router.reshape(N_EXPERTS)])
    bias = jnp.pad(bias, (0, ROUTER_LANES - bias.shape[0])).reshape(1, ROUTER_LANES)
    w_hi = w.astype(BF16)
    w_lo = (w - w_hi.astype(F32)).astype(BF16)
    return w_hi, w_lo, bias.astype(F32)


def _band_bias_t(rel_bias):
    m = jnp.arange(PAIR_BAND + PAIR - 1) - (PAIR - 1) - LEFT
    ext = rel_bias.astype(F32)[:, jnp.clip(m, -MAX_REL, MAX_REL) + MAX_REL]
    bias = jnp.stack([ext[:, PAIR - 1 - qq:PAIR - 1 - qq + PAIR_BAND] for qq in range(PAIR)], axis=1)
    first = (jnp.arange(PAIR) // CHUNK * CHUNK)[:, None]
    kk = jnp.arange(PAIR_BAND)[None, :]
    bias = jnp.where(((kk >= first) & (kk < first + BAND))[None], bias * LOG2E, -1e30)
    bias_t = jnp.transpose(bias, (0, 2, 1)).reshape(ATT_HEADS // 2, 2, PAIR_BAND, PAIR)
    return jnp.transpose(bias_t, (0, 2, 1, 3)).reshape(ATT_HEADS // 2, PAIR_BAND, 2 * PAIR)


def kernel(x, norm_mix_g, norm_ffn_g, final_norm_g, ab_w_in, pool_w, pool_scale, att_rel_bias,
           ab_w_out, sgu_w_in, sgu_b_in, sgu_ln_g, sgu_ln_b, sgu_w_s, sgu_b_s, sgu_w_out,
           moe_wg_router, moe_bg_router, moe_we_router, moe_be_router,
           moe_w_gate, moe_w_up, moe_w_down):
    b, s, d = x.shape
    row = lambda a: a.reshape(1, -1).astype(F32)

    def moe(h, layer, final_norm):
        wr_hi, wr_lo, br = _router_params(moe_wg_router[layer], moe_bg_router[layer],
                                          moe_we_router[layer], moe_be_router[layer])
        return _moe(h, row(norm_ffn_g[layer]), wr_hi, wr_lo, br,
                    moe_w_gate[layer].astype(BF16), moe_w_up[layer].astype(BF16),
                    moe_w_down[layer].astype(BF16), row(final_norm_g), final_norm)

    w_in = ab_w_in[0]
    n_pqk = POOL_WIDTH + 2 * ATT_WIDTH
    p, q, k, vt = _inproj(x, row(norm_mix_g[0]), w_in[:, :n_pqk].astype(BF16),
                          w_in[:, n_pqk:].T.astype(BF16))
    h = _mix(x, p, q, k, vt, _band_bias_t(att_rel_bias[0]), pool_w[0].astype(BF16),
             row(pool_scale[0]), ab_w_out[0].astype(BF16))
    h = moe(h.reshape(b * s, d), 0, False)

    h = _sgu(h, row(norm_mix_g[1]), sgu_w_in[0].astype(BF16), row(sgu_b_in[0]),
             row(sgu_ln_g[0]), row(sgu_ln_b[0]), sgu_w_s[0].astype(BF16),
             sgu_b_s[0].astype(F32)[:, :, None], sgu_w_out[0].astype(BF16))
    h = moe(h, 1, True)
    return h.reshape(b, s, d)
```

```python
import functools

import jax
import jax.numpy as jnp
from jax import lax
from jax.experimental import pallas as pl
from jax.experimental.pallas import tpu as pltpu

EPS = 1e-6
CHUNK = 64
POOL_WINDOWS = (2, 4, 8, 16)
POOL_GROUP_DIM = 128
POOL_WIDTH = 512
POOL_HALO = 16
ATT_HEADS = 8
ATT_HEAD_DIM = 64
ATT_WIDTH = 512
LEFT_CHUNKS = 8
LEFT = LEFT_CHUNKS * CHUNK
BAND = LEFT + CHUNK
PAIR = 2 * CHUNK
PAIR_BAND = LEFT + PAIR
MAX_REL = 128
ATT_SCALE = ATT_HEAD_DIM ** -0.5
LOG2E = 1.4426950408889634
VT_ONES = 16
VT_PAIR = 2 * ATT_HEAD_DIM + VT_ONES
VT_ROWS = (ATT_HEADS // 2) * VT_PAIR
SGU_CHUNK = 128
SGU_HEADS = 8
N_GROUPS = 4
EXPERTS_PER_GROUP = 4
N_EXPERTS = 16
D_EXPERT = 256
ROUTER_LANES = 128

VMEM_LIMIT = 56 * 1024 * 1024

F32 = jnp.float32
BF16 = jnp.bfloat16


def _rms(x, g):
    return x * lax.rsqrt(jnp.mean(x * x, axis=-1, keepdims=True) + EPS) * g


def _inproj_kernel(x_ref, g_ref, w_ref, wvt_ref, p_ref, q_ref, k_ref, vt_ref):
    i = pl.program_id(1)

    @pl.when(i == 0)
    def _():
        k_ref[...] = jnp.zeros_like(k_ref)
        vt_ref[...] = jnp.zeros_like(vt_ref)

    @pl.when(i > 0)
    def _():
        xn = _rms(x_ref[0], g_ref[...]).astype(BF16)
        z = jnp.dot(xn, w_ref[...], preferred_element_type=F32)
        p_ref[0] = z[:, :POOL_WIDTH]
        q_ref[0] = (z[:, POOL_WIDTH:POOL_WIDTH + ATT_WIDTH] * (ATT_SCALE * LOG2E)).astype(BF16)
        k_ref[0] = z[:, POOL_WIDTH + ATT_WIDTH:].astype(BF16)
        vt = lax.dot_general(wvt_ref[...], xn, (((1,), (1,)), ((), ())), preferred_element_type=F32)
        for j in range(ATT_HEADS // 2):
            rows = slice(2 * j * ATT_HEAD_DIM, 2 * (j + 1) * ATT_HEAD_DIM)
            vt_ref[0, j * VT_PAIR:j * VT_PAIR + 2 * ATT_HEAD_DIM, :] = vt[rows].astype(BF16)
            vt_ref[0, j * VT_PAIR + 2 * ATT_HEAD_DIM:(j + 1) * VT_PAIR, :] = jnp.ones(
                (VT_ONES, vt.shape[1]), BF16)


def _inproj(x, g, w_pqk, w_vt):
    b, s, d = x.shape
    tm = LEFT
    nt = s // tm
    prev = lambda bi, i: (bi, jnp.maximum(i - 1, 0), 0)
    const2 = lambda bi, i: (0, 0)
    return pl.pallas_call(
        _inproj_kernel,
        grid=(b, nt + 1),
        in_specs=[
            pl.BlockSpec((1, tm, d), prev),
            pl.BlockSpec((1, d), const2),
            pl.BlockSpec(w_pqk.shape, const2),
            pl.BlockSpec(w_vt.shape, const2),
        ],
        out_specs=[
            pl.BlockSpec((1, tm, POOL_WIDTH), prev),
            pl.BlockSpec((1, tm, ATT_WIDTH), prev),
            pl.BlockSpec((1, tm, ATT_WIDTH), lambda bi, i: (bi, i, 0)),
            pl.BlockSpec((1, VT_ROWS, tm), lambda bi, i: (bi, 0, i)),
        ],
        out_shape=[
            jax.ShapeDtypeStruct((b, s, POOL_WIDTH), F32),
            jax.ShapeDtypeStruct((b, s, ATT_WIDTH), BF16),
            jax.ShapeDtypeStruct((b, s + LEFT, ATT_WIDTH), BF16),
            jax.ShapeDtypeStruct((b, VT_ROWS, s + LEFT), BF16),
        ],
        compiler_params=pltpu.CompilerParams(
            dimension_semantics=("arbitrary", "arbitrary"), vmem_limit_bytes=VMEM_LIMIT),
        name="inproj",
    )(x, g, w_pqk, w_vt)


def _attend_tile(q_ref, k_ref, vt_ref, biast_ref, mix, st_buf, i, tq, masked):
    lane = lax.broadcasted_iota(jnp.int32, (PAIR, 2 * ATT_HEAD_DIM), 1)
    even = lane < ATT_HEAD_DIM
    zero = jnp.zeros((), BF16)
    units = [(cp, j) for cp in range(tq // PAIR) for j in range(ATT_HEADS // 2)]

    def band_base(cp):
        return pl.multiple_of((i * (tq // PAIR) + cp) * PAIR, PAIR)

    def scores(u):
        cp, j = units[u]
        cols = slice(2 * j * ATT_HEAD_DIM, 2 * (j + 1) * ATT_HEAD_DIM)
        q2 = q_ref[0, cp * PAIR:(cp + 1) * PAIR, cols]
        qs = jnp.concatenate([jnp.where(even, q2, zero), jnp.where(even, zero, q2)], axis=0)
        kb = k_ref[0, pl.ds(band_base(cp), PAIR_BAND), cols]
        st_buf[u % 2] = lax.dot_general(kb, qs, (((1,), (1,)), ((), ())),
                                        preferred_element_type=F32)

    scores(0)
    for u, (cp, j) in enumerate(units):
        if u + 1 < len(units):
            scores(u + 1)
        cols = slice(2 * j * ATT_HEAD_DIM, 2 * (j + 1) * ATT_HEAD_DIM)
        base = band_base(cp)
        st = st_buf[u % 2] + biast_ref[j]
        if masked:
            valid = (lax.broadcasted_iota(jnp.int32, (PAIR_BAND, 1), 0) + base - LEFT) >= 0
            st = jnp.where(valid, st, -1e30)
        m = jnp.max(st, axis=0, keepdims=True)
        e = jnp.exp2(st - m).astype(BF16)
        vtb = vt_ref[0, j * VT_PAIR:(j + 1) * VT_PAIR, pl.ds(base, PAIR_BAND)]
        ot = jnp.dot(vtb, e, preferred_element_type=F32)
        ot = ot[:2 * ATT_HEAD_DIM] * (1.0 / ot[2 * ATT_HEAD_DIM:2 * ATT_HEAD_DIM + 1])
        o2t = jnp.concatenate([ot[:ATT_HEAD_DIM, :PAIR], ot[ATT_HEAD_DIM:, PAIR:]], axis=0)
        mix[cp * PAIR:(cp + 1) * PAIR, POOL_WIDTH + 2 * j * ATT_HEAD_DIM:
            POOL_WIDTH + 2 * (j + 1) * ATT_HEAD_DIM] = o2t.T.astype(BF16)


def _mix_kernel(x_ref, p_ref, halo_ref, q_ref, k_ref, vt_ref, biast_ref, wpool_ref, pscale_ref,
                wout_ref, o_ref, pbuf, mix, st_buf, *, tq):
    i = pl.program_id(1)

    halo = halo_ref[0]
    pbuf[0:POOL_HALO, :] = jnp.where(i == 0, jnp.zeros_like(halo), halo)
    pbuf[POOL_HALO:, :] = p_ref[0]
    t_abs = i * tq + lax.broadcasted_iota(jnp.int32, (tq, 1), 0)
    for gi, w in enumerate(POOL_WINDOWS):
        cols = slice(gi * POOL_GROUP_DIM, (gi + 1) * POOL_GROUP_DIM)
        cur = pbuf[POOL_HALO:, cols]
        acc = cur
        for j in range(1, w):
            acc = acc + pbuf[POOL_HALO - j:POOL_HALO - j + tq, cols]
        cnt = jnp.minimum(t_abs + 1, w).astype(F32)
        pooled = (acc / cnt - cur).astype(BF16)
        y = jnp.dot(pooled, wpool_ref[gi], preferred_element_type=F32) * pscale_ref[:, cols]
        mix[:, cols] = y.astype(BF16)

    def attend(masked):
        _attend_tile(q_ref, k_ref, vt_ref, biast_ref, mix, st_buf, i, tq, masked)

    pl.when(i * tq < LEFT)(lambda: attend(True))
    pl.when(i * tq >= LEFT)(lambda: attend(False))

    o_ref[0] = x_ref[0] + jnp.dot(mix[...], wout_ref[...], preferred_element_type=F32)


def _mix(x, p, q, k, vt, biast, w_pool, pool_scale, w_out):
    b, s, d = x.shape
    tq = 256
    halo_blocks = tq // POOL_HALO
    tile = lambda bi, i: (bi, i, 0)
    full = lambda bi, i: (bi, 0, 0)
    const2 = lambda bi, i: (0, 0)
    const3 = lambda bi, i: (0, 0, 0)
    return pl.pallas_call(
        functools.partial(_mix_kernel, tq=tq),
        grid=(b, s // tq),
        in_specs=[
            pl.BlockSpec((1, tq, d), tile),
            pl.BlockSpec((1, tq, POOL_WIDTH), tile),
            pl.BlockSpec((1, POOL_HALO, POOL_WIDTH),
                         lambda bi, i: (bi, jnp.maximum(i * halo_blocks - 1, 0), 0)),
            pl.BlockSpec((1, tq, ATT_WIDTH), tile),
            pl.BlockSpec((1, s + LEFT, ATT_WIDTH), full),
            pl.BlockSpec((1, VT_ROWS, s + LEFT), full),
            pl.BlockSpec(biast.shape, const3),
            pl.BlockSpec(w_pool.shape, const3),
            pl.BlockSpec(pool_scale.shape, const2),
            pl.BlockSpec(w_out.shape, const2),
        ],
        out_specs=pl.BlockSpec((1, tq, d), tile),
        out_shape=jax.ShapeDtypeStruct((b, s, d), F32),
        scratch_shapes=[
            pltpu.VMEM((POOL_HALO + tq, POOL_WIDTH), F32),
            pltpu.VMEM((tq, POOL_WIDTH + ATT_WIDTH), BF16),
            pltpu.VMEM((2, PAIR_BAND, 2 * PAIR), F32),
        ],
        compiler_params=pltpu.CompilerParams(
            dimension_semantics=("arbitrary", "arbitrary"), vmem_limit_bytes=VMEM_LIMIT),
        name="pool_attn_mix",
    )(x, p, p, q, k, vt, biast, w_pool, pool_scale, w_out)


def _sgu_kernel(h_ref, g_ref, win_ref, bin_ref, lng_ref, lnb_ref, ws_ref, bs_ref, wout_ref,
                o_ref, gated, *, tm):
    h = h_ref[...]
    d = h.shape[-1]
    xn = _rms(h, g_ref[...]).astype(BF16)
    z = jnp.dot(xn, win_ref[...], preferred_element_type=F32) + bin_ref[...]
    z = 0.5 * z * (1.0 + lax.erf(z * (2.0 ** -0.5)))
    u = z[:, :d]
    v = z[:, d:]
    mu = jnp.mean(v, axis=-1, keepdims=True)
    vc = v - mu
    var = jnp.mean(vc * vc, axis=-1, keepdims=True)
    vn = (vc * lax.rsqrt(var + EPS) * lng_ref[...] + lnb_ref[...]).astype(BF16)
    row = lax.broadcasted_iota(jnp.int32, (SGU_CHUNK, SGU_CHUNK), 0)
    col = lax.broadcasted_iota(jnp.int32, (SGU_CHUNK, SGU_CHUNK), 1)
    hd = d // SGU_HEADS
    for hh in range(SGU_HEADS):
        ws = jnp.where(row >= col, ws_ref[hh], jnp.zeros((), BF16))
        bs = bs_ref[hh]
        for n in range(tm // SGU_CHUNK):
            rows = slice(n * SGU_CHUNK, (n + 1) * SGU_CHUNK)
            cols = slice(hh * hd, (hh + 1) * hd)
            mixed = jnp.dot(ws, vn[rows, cols], preferred_element_type=F32) + bs
            gated[rows, cols] = (u[rows, cols] * mixed).astype(BF16)
    o_ref[...] = h + jnp.dot(gated[...], wout_ref[...], preferred_element_type=F32)


def _sgu(h, g, w_in, b_in, ln_g, ln_b, w_s, b_s, w_out):
    n, d = h.shape
    tm = 256
    tile = lambda i: (i, 0)
    const2 = lambda i: (0, 0)
    const3 = lambda i: (0, 0, 0)
    return pl.pallas_call(
        functools.partial(_sgu_kernel, tm=tm),
        grid=(n // tm,),
        in_specs=[
            pl.BlockSpec((tm, d), tile),
            pl.BlockSpec((1, d), const2),
            pl.BlockSpec(w_in.shape, const2),
            pl.BlockSpec(b_in.shape, const2),
            pl.BlockSpec(ln_g.shape, const2),
            pl.BlockSpec(ln_b.shape, const2),
            pl.BlockSpec(w_s.shape, const3),
            pl.BlockSpec(b_s.shape, const3),
            pl.BlockSpec(w_out.shape, const2),
        ],
        out_specs=pl.BlockSpec((tm, d), tile),
        out_shape=jax.ShapeDtypeStruct((n, d), F32),
        scratch_shapes=[pltpu.VMEM((tm, d), BF16)],
        compiler_params=pltpu.CompilerParams(
            dimension_semantics=("arbitrary",), vmem_limit_bytes=VMEM_LIMIT),
        name="sgu",
    )(h, g, w_in, b_in, ln_g, ln_b, w_s, b_s, w_out)


def _route(logits):
    lane = lax.broadcasted_iota(jnp.int32, logits.shape, 1)
    big = jnp.int32(ROUTER_LANES)
    neg = jnp.float32(-jnp.inf)
    gl = jnp.where(lane < N_GROUPS, logits, neg)
    gmax = jnp.max(gl, axis=-1, keepdims=True)
    gidx = jnp.min(jnp.where(gl == gmax, lane, big), axis=-1, keepdims=True)
    g_w = 1.0 / jnp.sum(jnp.exp(gl - gmax), axis=-1, keepdims=True)
    lo = N_GROUPS + EXPERTS_PER_GROUP * gidx
    el = jnp.where((lane >= lo) & (lane < lo + EXPERTS_PER_GROUP), logits, neg)
    m1 = jnp.max(el, axis=-1, keepdims=True)
    i1 = jnp.min(jnp.where(el == m1, lane, big), axis=-1, keepdims=True)
    el2 = jnp.where(lane == i1, neg, el)
    m2 = jnp.max(el2, axis=-1, keepdims=True)
    i2 = jnp.min(jnp.where(el2 == m2, lane, big), axis=-1, keepdims=True)
    e2 = jnp.exp(m2 - m1)
    den = 1.0 + e2
    w1 = g_w / den
    w2 = g_w * e2 / den
    return jnp.where(lane == i1, w1, 0.0) + jnp.where(lane == i2, w2, 0.0)


def _moe_kernel(h_ref, g_ref, wr_ref, br_ref, wgu_ref, wd_ref, gf_ref,
                o_ref, xn_ref, comb_ref, hgu_buf, act_ref, *, final_norm):
    grp = pl.program_id(1)

    @pl.when(grp == 0)
    def _():
        h = h_ref[...]
        xn = _rms(h, g_ref[...])
        x_hi = xn.astype(BF16)
        x_lo = (xn - x_hi.astype(F32)).astype(BF16)
        both = jnp.dot(x_hi, wr_ref[...], preferred_element_type=F32)
        logits = (both[:, :ROUTER_LANES] + both[:, ROUTER_LANES:]
                  + jnp.dot(x_lo, wr_ref[:, :ROUTER_LANES], preferred_element_type=F32)) + br_ref[...]
        comb_ref[...] = _route(logits)
        xn_ref[...] = x_hi
        o_ref[...] = h

    lane = lax.broadcasted_iota(jnp.int32, comb_ref.shape, 1)
    first = N_GROUPS + grp * EXPERTS_PER_GROUP

    def gate_up(j):
        hgu_buf[j % 2] = jnp.dot(xn_ref[...], wgu_ref[0, :, 2 * j * D_EXPERT:2 * (j + 1) * D_EXPERT],
                                 preferred_element_type=F32)

    gate_up(0)
    for j in range(EXPERTS_PER_GROUP):
        if j + 1 < EXPERTS_PER_GROUP:
            gate_up(j + 1)
        c = jnp.sum(jnp.where(lane == first + j, comb_ref[...], 0.0), axis=-1, keepdims=True)
        hg = hgu_buf[j % 2, :, :D_EXPERT]
        hu = hgu_buf[j % 2, :, D_EXPERT:]
        act = hg * (1.0 / (1.0 + jnp.exp(-hg))) * hu * c
        act_ref[:, j * D_EXPERT:(j + 1) * D_EXPERT] = act.astype(BF16)
    o_ref[...] += jnp.dot(act_ref[...], wd_ref[0], preferred_element_type=F32)

    if final_norm:
        @pl.when(grp == N_GROUPS - 1)
        def _():
            o_ref[...] = _rms(o_ref[...], gf_ref[...])


def _moe(h, g, wr, br, wgu, wd, gf, final_norm):
    n, d = h.shape
    tm = 1024
    tile = lambda i, e: (i, 0)
    const2 = lambda i, e: (0, 0)
    group = lambda i, e: (e, 0, 0)
    return pl.pallas_call(
        functools.partial(_moe_kernel, final_norm=final_norm),
        grid=(n // tm, N_GROUPS),
        in_specs=[
            pl.BlockSpec((tm, d), tile),
            pl.BlockSpec((1, d), const2),
            pl.BlockSpec(wr.shape, const2),
            pl.BlockSpec(br.shape, const2),
            pl.BlockSpec((1,) + wgu.shape[1:], group),
            pl.BlockSpec((1,) + wd.shape[1:], group),
            pl.BlockSpec((1, d), const2),
        ],
        out_specs=pl.BlockSpec((tm, d), tile),
        out_shape=jax.ShapeDtypeStruct((n, d), F32),
        scratch_shapes=[
            pltpu.VMEM((tm, d), BF16),
            pltpu.VMEM((tm, ROUTER_LANES), F32),
            pltpu.VMEM((2, tm, 2 * D_EXPERT), F32),
            pltpu.VMEM((tm, EXPERTS_PER_GROUP * D_EXPERT), BF16),
        ],
        compiler_params=pltpu.CompilerParams(
            dimension_semantics=("arbitrary", "arbitrary"), vmem_limit_bytes=VMEM_LIMIT),
        name="hier_moe",
    )(h, g, wr, br, wgu, wd, gf)


def _router_params(wg_router, bg_router, we_router, be_router):
    d = wg_router.shape[0]
    we = jnp.transpose(we_router, (1, 0, 2)).reshape(d, N_EXPERTS)
    w = jnp.concatenate([wg_router, we], axis=1)
    w = jnp.pad(w, ((0, 0), (0, ROUTER_LANES - w.shape[1])))
    bias = jnp.concatenate([bg_router, be_router.reshape(N_EXPERTS)])
    bias = jnp.pad(bias, (0, ROUTER_LANES - bias.shape[0])).reshape(1, ROUTER_LANES)
    w_hi = w.astype(BF16)
    w_lo = (w - w_hi.astype(F32)).astype(BF16)
    return jnp.concatenate([w_hi, w_lo], axis=1), bias.astype(F32)


def _expert_params(w_gate, w_up, w_down):
    e, d, f = w_gate.shape
    wgu = jnp.concatenate([w_gate, w_up], axis=2).astype(BF16)
    wgu = wgu.reshape(N_GROUPS, EXPERTS_PER_GROUP, d, 2 * f)
    wgu = jnp.transpose(wgu, (0, 2, 1, 3)).reshape(N_GROUPS, d, EXPERTS_PER_GROUP * 2 * f)
    wd = w_down.astype(BF16).reshape(N_GROUPS, EXPERTS_PER_GROUP * f, d)
    return wgu, wd


def _band_bias_t(rel_bias):
    m = jnp.arange(PAIR_BAND + PAIR - 1) - (PAIR - 1) - LEFT
    ext = rel_bias.astype(F32)[:, jnp.clip(m, -MAX_REL, MAX_REL) + MAX_REL]
    bias = jnp.stack([ext[:, PAIR - 1 - qq:PAIR - 1 - qq + PAIR_BAND] for qq in range(PAIR)], axis=1)
    first = (jnp.arange(PAIR) // CHUNK * CHUNK)[:, None]
    kk = jnp.arange(PAIR_BAND)[None, :]
    bias = jnp.where(((kk >= first) & (kk < first + BAND))[None], bias * LOG2E, -1e30)
    bias_t = jnp.transpose(bias, (0, 2, 1)).reshape(ATT_HEADS // 2, 2, PAIR_BAND, PAIR)
    return jnp.transpose(bias_t, (0, 2, 1, 3)).reshape(ATT_HEADS // 2, PAIR_BAND, 2 * PAIR)


def kernel(x, norm_mix_g, norm_ffn_g, final_norm_g, ab_w_in, pool_w, pool_scale, att_rel_bias,
           ab_w_out, sgu_w_in, sgu_b_in, sgu_ln_g, sgu_ln_b, sgu_w_s, sgu_b_s, sgu_w_out,
           moe_wg_router, moe_bg_router, moe_we_router, moe_be_router,
           moe_w_gate, moe_w_up, moe_w_down):
    b, s, d = x.shape
    row = lambda a: a.reshape(1, -1).astype(F32)

    def moe(h, layer, final_norm):
        wr, br = _router_params(moe_wg_router[layer], moe_bg_router[layer],
                                moe_we_router[layer], moe_be_router[layer])
        wgu, wd = _expert_params(moe_w_gate[layer], moe_w_up[layer], moe_w_down[layer])
        return _moe(h, row(norm_ffn_g[layer]), wr, br, wgu, wd, row(final_norm_g), final_norm)

    w_in = ab_w_in[0]
    n_pqk = POOL_WIDTH + 2 * ATT_WIDTH
    p, q, k, vt = _inproj(x, row(norm_mix_g[0]), w_in[:, :n_pqk].astype(BF16),
                          w_in[:, n_pqk:].T.astype(BF16))
    h = _mix(x, p, q, k, vt, _band_bias_t(att_rel_bias[0]), pool_w[0].astype(BF16),
             row(pool_scale[0]), ab_w_out[0].astype(BF16))
    h = moe(h.reshape(b * s, d), 0, False)

    h = _sgu(h, row(norm_mix_g[1]), sgu_w_in[0].astype(BF16), row(sgu_b_in[0]),
             row(sgu_ln_g[0]), row(sgu_ln_b[0]), sgu_w_s[0].astype(BF16),
             sgu_b_s[0].astype(F32)[:, :, None], sgu_w_out[0].astype(BF16))
    h = moe(h, 1, True)
    return h.reshape(b, s, d)
```

```python
import functools

import jax
import jax.numpy as jnp
from jax import lax
from jax.experimental import pallas as pl
from jax.experimental.pallas import tpu as pltpu

EPS = 1e-6
CHUNK = 64
POOL_WINDOWS = (2, 4, 8, 16)
POOL_GROUP_DIM = 128
POOL_WIDTH = 512
POOL_HALO = 16
ATT_HEADS = 8
ATT_HEAD_DIM = 64
ATT_WIDTH = 512
LEFT_CHUNKS = 8
LEFT = LEFT_CHUNKS * CHUNK
BAND = LEFT + CHUNK
PAIR = 2 * CHUNK
PAIR_BAND = LEFT + PAIR
MAX_REL = 128
ATT_SCALE = ATT_HEAD_DIM ** -0.5
LOG2E = 1.4426950408889634
VT_ONES = 16
VT_PAIR = 2 * ATT_HEAD_DIM + VT_ONES
VT_ROWS = (ATT_HEADS // 2) * VT_PAIR
SGU_CHUNK = 128
SGU_HEADS = 8
N_GROUPS = 4
EXPERTS_PER_GROUP = 4
N_EXPERTS = 16
D_EXPERT = 256
ROUTER_LANES = 128

VMEM_LIMIT = 56 * 1024 * 1024

F32 = jnp.float32
BF16 = jnp.bfloat16


def _rms(x, g):
    return x * lax.rsqrt(jnp.mean(x * x, axis=-1, keepdims=True) + EPS) * g


def _inproj_kernel(x_ref, g_ref, w_ref, wvt_ref, p_ref, q_ref, k_ref, vt_ref):
    i = pl.program_id(1)

    @pl.when(i == 0)
    def _():
        k_ref[...] = jnp.zeros_like(k_ref)
        vt_ref[...] = jnp.zeros_like(vt_ref)

    @pl.when(i > 0)
    def _():
        xn = _rms(x_ref[0], g_ref[...]).astype(BF16)
        z = jnp.dot(xn, w_ref[...], preferred_element_type=F32)
        p_ref[0] = z[:, :POOL_WIDTH]
        q_ref[0] = (z[:, POOL_WIDTH:POOL_WIDTH + ATT_WIDTH] * (ATT_SCALE * LOG2E)).astype(BF16)
        k_ref[0] = z[:, POOL_WIDTH + ATT_WIDTH:].astype(BF16)
        vt = lax.dot_general(wvt_ref[...], xn, (((1,), (1,)), ((), ())), preferred_element_type=F32)
        for j in range(ATT_HEADS // 2):
            rows = slice(2 * j * ATT_HEAD_DIM, 2 * (j + 1) * ATT_HEAD_DIM)
            vt_ref[0, j * VT_PAIR:j * VT_PAIR + 2 * ATT_HEAD_DIM, :] = vt[rows].astype(BF16)
            vt_ref[0, j * VT_PAIR + 2 * ATT_HEAD_DIM:(j + 1) * VT_PAIR, :] = jnp.ones(
                (VT_ONES, vt.shape[1]), BF16)


def _inproj(x, g, w_pqk, w_vt):
    b, s, d = x.shape
    tm = LEFT
    nt = s // tm
    prev = lambda bi, i: (bi, jnp.maximum(i - 1, 0), 0)
    const2 = lambda bi, i: (0, 0)
    return pl.pallas_call(
        _inproj_kernel,
        grid=(b, nt + 1),
        in_specs=[
            pl.BlockSpec((1, tm, d), prev),
            pl.BlockSpec((1, d), const2),
            pl.BlockSpec(w_pqk.shape, const2),
            pl.BlockSpec(w_vt.shape, const2),
        ],
        out_specs=[
            pl.BlockSpec((1, tm, POOL_WIDTH), prev),
            pl.BlockSpec((1, tm, ATT_WIDTH), prev),
            pl.BlockSpec((1, tm, ATT_WIDTH), lambda bi, i: (bi, i, 0)),
            pl.BlockSpec((1, VT_ROWS, tm), lambda bi, i: (bi, 0, i)),
        ],
        out_shape=[
            jax.ShapeDtypeStruct((b, s, POOL_WIDTH), F32),
            jax.ShapeDtypeStruct((b, s, ATT_WIDTH), BF16),
            jax.ShapeDtypeStruct((b, s + LEFT, ATT_WIDTH), BF16),
            jax.ShapeDtypeStruct((b, VT_ROWS, s + LEFT), BF16),
        ],
        compiler_params=pltpu.CompilerParams(
            dimension_semantics=("arbitrary", "arbitrary"), vmem_limit_bytes=VMEM_LIMIT),
        name="inproj",
    )(x, g, w_pqk, w_vt)


def _attend_tile(q_ref, k_ref, vt_ref, biast_ref, mix, st_buf, i, tq, masked):
    lane = lax.broadcasted_iota(jnp.int32, (PAIR, 2 * ATT_HEAD_DIM), 1)
    even = lane < ATT_HEAD_DIM
    zero = jnp.zeros((), BF16)
    units = [(cp, j) for cp in range(tq // PAIR) for j in range(ATT_HEADS // 2)]

    def band_base(cp):
        return pl.multiple_of((i * (tq // PAIR) + cp) * PAIR, PAIR)

    def scores(u):
        cp, j = units[u]
        cols = slice(2 * j * ATT_HEAD_DIM, 2 * (j + 1) * ATT_HEAD_DIM)
        q2 = q_ref[0, cp * PAIR:(cp + 1) * PAIR, cols]
        qs = jnp.concatenate([jnp.where(even, q2, zero), jnp.where(even, zero, q2)], axis=0)
        kb = k_ref[0, pl.ds(band_base(cp), PAIR_BAND), cols]
        st_buf[u % 2] = lax.dot_general(kb, qs, (((1,), (1,)), ((), ())),
                                        preferred_element_type=F32)

    scores(0)
    for u, (cp, j) in enumerate(units):
        if u + 1 < len(units):
            scores(u + 1)
        cols = slice(2 * j * ATT_HEAD_DIM, 2 * (j + 1) * ATT_HEAD_DIM)
        base = band_base(cp)
        st = st_buf[u % 2] + biast_ref[j]
        if masked:
            valid = (lax.broadcasted_iota(jnp.int32, (PAIR_BAND, 1), 0) + base - LEFT) >= 0
            st = jnp.where(valid, st, -1e30)
        m = jnp.max(st, axis=0, keepdims=True)
        e = jnp.exp2(st - m).astype(BF16)
        vtb = vt_ref[0, j * VT_PAIR:(j + 1) * VT_PAIR, pl.ds(base, PAIR_BAND)]
        ot = jnp.dot(vtb, e, preferred_element_type=F32)
        ot = ot[:2 * ATT_HEAD_DIM] * (1.0 / ot[2 * ATT_HEAD_DIM:2 * ATT_HEAD_DIM + 1])
        o2t = jnp.concatenate([ot[:ATT_HEAD_DIM, :PAIR], ot[ATT_HEAD_DIM:, PAIR:]], axis=0)
        mix[cp * PAIR:(cp + 1) * PAIR, POOL_WIDTH + 2 * j * ATT_HEAD_DIM:
            POOL_WIDTH + 2 * (j + 1) * ATT_HEAD_DIM] = o2t.T.astype(BF16)


def _mix_kernel(x_ref, p_ref, halo_ref, q_ref, k_ref, vt_ref, biast_ref, wpool_ref, pscale_ref,
                wout_ref, o_ref, pbuf, mix, st_buf, *, tq):
    i = pl.program_id(1)

    halo = halo_ref[0]
    pbuf[0:POOL_HALO, :] = jnp.where(i == 0, jnp.zeros_like(halo), halo)
    pbuf[POOL_HALO:, :] = p_ref[0]
    t_abs = i * tq + lax.broadcasted_iota(jnp.int32, (tq, 1), 0)
    for gi, w in enumerate(POOL_WINDOWS):
        cols = slice(gi * POOL_GROUP_DIM, (gi + 1) * POOL_GROUP_DIM)
        cur = pbuf[POOL_HALO:, cols]
        acc = cur
        for j in range(1, w):
            acc = acc + pbuf[POOL_HALO - j:POOL_HALO - j + tq, cols]
        cnt = jnp.minimum(t_abs + 1, w).astype(F32)
        pooled = (acc / cnt - cur).astype(BF16)
        y = jnp.dot(pooled, wpool_ref[gi], preferred_element_type=F32) * pscale_ref[:, cols]
        mix[:, cols] = y.astype(BF16)

    def attend(masked):
        _attend_tile(q_ref, k_ref, vt_ref, biast_ref, mix, st_buf, i, tq, masked)

    pl.when(i * tq < LEFT)(lambda: attend(True))
    pl.when(i * tq >= LEFT)(lambda: attend(False))

    o_ref[0] = x_ref[0] + jnp.dot(mix[...], wout_ref[...], preferred_element_type=F32)


def _mix(x, p, q, k, vt, biast, w_pool, pool_scale, w_out):
    b, s, d = x.shape
    tq = 256
    halo_blocks = tq // POOL_HALO
    tile = lambda bi, i: (bi, i, 0)
    full = lambda bi, i: (bi, 0, 0)
    const2 = lambda bi, i: (0, 0)
    const3 = lambda bi, i: (0, 0, 0)
    return pl.pallas_call(
        functools.partial(_mix_kernel, tq=tq),
        grid=(b, s // tq),
        in_specs=[
            pl.BlockSpec((1, tq, d), tile),
            pl.BlockSpec((1, tq, POOL_WIDTH), tile),
            pl.BlockSpec((1, POOL_HALO, POOL_WIDTH),
                         lambda bi, i: (bi, jnp.maximum(i * halo_blocks - 1, 0), 0)),
            pl.BlockSpec((1, tq, ATT_WIDTH), tile),
            pl.BlockSpec((1, s + LEFT, ATT_WIDTH), full),
            pl.BlockSpec((1, VT_ROWS, s + LEFT), full),
            pl.BlockSpec(biast.shape, const3),
            pl.BlockSpec(w_pool.shape, const3),
            pl.BlockSpec(pool_scale.shape, const2),
            pl.BlockSpec(w_out.shape, const2),
        ],
        out_specs=pl.BlockSpec((1, tq, d), tile),
        out_shape=jax.ShapeDtypeStruct((b, s, d), F32),
        scratch_shapes=[
            pltpu.VMEM((POOL_HALO + tq, POOL_WIDTH), F32),
            pltpu.VMEM((tq, POOL_WIDTH + ATT_WIDTH), BF16),
            pltpu.VMEM((2, PAIR_BAND, 2 * PAIR), F32),
        ],
        compiler_params=pltpu.CompilerParams(
            dimension_semantics=("arbitrary", "arbitrary"), vmem_limit_bytes=VMEM_LIMIT),
        name="pool_attn_mix",
    )(x, p, p, q, k, vt, biast, w_pool, pool_scale, w_out)


def _sgu_kernel(h_ref, g_ref, win_ref, bin_ref, lng_ref, lnb_ref, ws_ref, bs_ref, wout_ref,
                o_ref, xn_ref, z_buf, gated, *, tm, sub):
    d = h_ref.shape[-1]
    hd = d // SGU_HEADS
    xn_ref[...] = _rms(h_ref[...], g_ref[...]).astype(BF16)
    row = lax.broadcasted_iota(jnp.int32, (SGU_CHUNK, SGU_CHUNK), 0)
    col = lax.broadcasted_iota(jnp.int32, (SGU_CHUNK, SGU_CHUNK), 1)
    ws = [jnp.where(row >= col, ws_ref[hh], jnp.zeros((), BF16)) for hh in range(SGU_HEADS)]

    def in_proj(s):
        z_buf[s % 2] = jnp.dot(xn_ref[s * sub:(s + 1) * sub, :], win_ref[...],
                               preferred_element_type=F32)

    in_proj(0)
    for s in range(tm // sub):
        if s + 1 < tm // sub:
            in_proj(s + 1)
        z = z_buf[s % 2] + bin_ref[...]
        z = 0.5 * z * (1.0 + lax.erf(z * (2.0 ** -0.5)))
        u = z[:, :d]
        v = z[:, d:]
        mu = jnp.mean(v, axis=-1, keepdims=True)
        vc = v - mu
        var = jnp.mean(vc * vc, axis=-1, keepdims=True)
        vn = (vc * lax.rsqrt(var + EPS) * lng_ref[...] + lnb_ref[...]).astype(BF16)
        for hh in range(SGU_HEADS):
            cols = slice(hh * hd, (hh + 1) * hd)
            for n in range(sub // SGU_CHUNK):
                rows = slice(n * SGU_CHUNK, (n + 1) * SGU_CHUNK)
                mixed = jnp.dot(ws[hh], vn[rows, cols], preferred_element_type=F32) + bs_ref[hh]
                gated[s * sub + n * SGU_CHUNK:s * sub + (n + 1) * SGU_CHUNK, cols] = (
                    u[rows, cols] * mixed).astype(BF16)
        blk = slice(s * sub, (s + 1) * sub)
        o_ref[blk, :] = h_ref[blk, :] + jnp.dot(gated[blk, :], wout_ref[...],
                                                preferred_element_type=F32)


def _sgu(h, g, w_in, b_in, ln_g, ln_b, w_s, b_s, w_out):
    n, d = h.shape
    tm, sub = 1024, 256
    tile = lambda i: (i, 0)
    const2 = lambda i: (0, 0)
    const3 = lambda i: (0, 0, 0)
    return pl.pallas_call(
        functools.partial(_sgu_kernel, tm=tm, sub=sub),
        grid=(n // tm,),
        in_specs=[
            pl.BlockSpec((tm, d), tile),
            pl.BlockSpec((1, d), const2),
            pl.BlockSpec(w_in.shape, const2),
            pl.BlockSpec(b_in.shape, const2),
            pl.BlockSpec(ln_g.shape, const2),
            pl.BlockSpec(ln_b.shape, const2),
            pl.BlockSpec(w_s.shape, const3),
            pl.BlockSpec(b_s.shape, const3),
            pl.BlockSpec(w_out.shape, const2),
        ],
        out_specs=pl.BlockSpec((tm, d), tile),
        out_shape=jax.ShapeDtypeStruct((n, d), F32),
        scratch_shapes=[
            pltpu.VMEM((tm, d), BF16),
            pltpu.VMEM((2, sub, 2 * d), F32),
            pltpu.VMEM((tm, d), BF16),
        ],
        compiler_params=pltpu.CompilerParams(
            dimension_semantics=("arbitrary",), vmem_limit_bytes=VMEM_LIMIT),
        name="sgu",
    )(h, g, w_in, b_in, ln_g, ln_b, w_s, b_s, w_out)


def _route(logits):
    lane = lax.broadcasted_iota(jnp.int32, logits.shape, 1)
    big = jnp.int32(ROUTER_LANES)
    neg = jnp.float32(-jnp.inf)
    gl = jnp.where(lane < N_GROUPS, logits, neg)
    gmax = jnp.max(gl, axis=-1, keepdims=True)
    gidx = jnp.min(jnp.where(gl == gmax, lane, big), axis=-1, keepdims=True)
    g_w = 1.0 / jnp.sum(jnp.exp(gl - gmax), axis=-1, keepdims=True)
    lo = N_GROUPS + EXPERTS_PER_GROUP * gidx
    el = jnp.where((lane >= lo) & (lane < lo + EXPERTS_PER_GROUP), logits, neg)
    m1 = jnp.max(el, axis=-1, keepdims=True)
    i1 = jnp.min(jnp.where(el == m1, lane, big), axis=-1, keepdims=True)
    el2 = jnp.where(lane == i1, neg, el)
    m2 = jnp.max(el2, axis=-1, keepdims=True)
    i2 = jnp.min(jnp.where(el2 == m2, lane, big), axis=-1, keepdims=True)
    e2 = jnp.exp(m2 - m1)
    den = 1.0 + e2
    w1 = g_w / den
    w2 = g_w * e2 / den
    return jnp.where(lane == i1, w1, 0.0) + jnp.where(lane == i2, w2, 0.0)


def _moe_kernel(h_ref, g_ref, wr_ref, br_ref, wg_ref, wu_ref, wd_ref, gf_ref,
                o_ref, xn_ref, comb_ref, hg_buf, hu_buf, act_ref, *, final_norm):
    grp = pl.program_id(1)

    @pl.when(grp == 0)
    def _():
        h = h_ref[...]
        xn = _rms(h, g_ref[...])
        x_hi = xn.astype(BF16)
        x_lo = (xn - x_hi.astype(F32)).astype(BF16)
        both = jnp.dot(x_hi, wr_ref[...], preferred_element_type=F32)
        logits = (both[:, :ROUTER_LANES] + both[:, ROUTER_LANES:]
                  + jnp.dot(x_lo, wr_ref[:, :ROUTER_LANES], preferred_element_type=F32)) + br_ref[...]
        comb_ref[...] = _route(logits)
        xn_ref[...] = x_hi
        o_ref[...] = h

    lane = lax.broadcasted_iota(jnp.int32, comb_ref.shape, 1)
    first = N_GROUPS + grp * EXPERTS_PER_GROUP

    def gate_up(j):
        hg_buf[j % 2] = jnp.dot(xn_ref[...], wg_ref[j], preferred_element_type=F32)
        hu_buf[j % 2] = jnp.dot(xn_ref[...], wu_ref[j], preferred_element_type=F32)

    gate_up(0)
    for j in range(EXPERTS_PER_GROUP):
        if j + 1 < EXPERTS_PER_GROUP:
            gate_up(j + 1)
        c = jnp.sum(jnp.where(lane == first + j, comb_ref[...], 0.0), axis=-1, keepdims=True)
        hg = hg_buf[j % 2]
        hu = hu_buf[j % 2]
        act = hg * (1.0 / (1.0 + jnp.exp(-hg))) * hu * c
        act_ref[:, j * D_EXPERT:(j + 1) * D_EXPERT] = act.astype(BF16)
    o_ref[...] += jnp.dot(act_ref[...], wd_ref[0], preferred_element_type=F32)

    if final_norm:
        @pl.when(grp == N_GROUPS - 1)
        def _():
            o_ref[...] = _rms(o_ref[...], gf_ref[...])


def _moe(h, g, wr, br, wg, wu, wd, gf, final_norm):
    n, d = h.shape
    tm = 1024
    tile = lambda i, e: (i, 0)
    const2 = lambda i, e: (0, 0)
    group = lambda i, e: (e, 0, 0)
    return pl.pallas_call(
        functools.partial(_moe_kernel, final_norm=final_norm),
        grid=(n // tm, N_GROUPS),
        in_specs=[
            pl.BlockSpec((tm, d), tile),
            pl.BlockSpec((1, d), const2),
            pl.BlockSpec(wr.shape, const2),
            pl.BlockSpec(br.shape, const2),
            pl.BlockSpec((EXPERTS_PER_GROUP,) + wg.shape[1:], group),
            pl.BlockSpec((EXPERTS_PER_GROUP,) + wu.shape[1:], group),
            pl.BlockSpec((1,) + wd.shape[1:], group),
            pl.BlockSpec((1, d), const2),
        ],
        out_specs=pl.BlockSpec((tm, d), tile),
        out_shape=jax.ShapeDtypeStruct((n, d), F32),
        scratch_shapes=[
            pltpu.VMEM((tm, d), BF16),
            pltpu.VMEM((tm, ROUTER_LANES), F32),
            pltpu.VMEM((2, tm, D_EXPERT), F32),
            pltpu.VMEM((2, tm, D_EXPERT), F32),
            pltpu.VMEM((tm, EXPERTS_PER_GROUP * D_EXPERT), BF16),
        ],
        compiler_params=pltpu.CompilerParams(
            dimension_semantics=("arbitrary", "arbitrary"), vmem_limit_bytes=VMEM_LIMIT),
        name="hier_moe",
    )(h, g, wr, br, wg, wu, wd, gf)


def _router_params(wg_router, bg_router, we_router, be_router):
    d = wg_router.shape[0]
    we = jnp.transpose(we_router, (1, 0, 2)).reshape(d, N_EXPERTS)
    w = jnp.concatenate([wg_router, we], axis=1)
    w = jnp.pad(w, ((0, 0), (0, ROUTER_LANES - w.shape[1])))
    bias = jnp.concatenate([bg_router, be_router.reshape(N_EXPERTS)])
    bias = jnp.pad(bias, (0, ROUTER_LANES - bias.shape[0])).reshape(1, ROUTER_LANES)
    w_hi = w.astype(BF16)
    w_lo = (w - w_hi.astype(F32)).astype(BF16)
    return jnp.concatenate([w_hi, w_lo], axis=1), bias.astype(F32)


def _band_bias_t(rel_bias):
    width = PAIR_BAND + PAIR - 1
    m = jnp.arange(width + 1) - (PAIR - 1) - LEFT
    ext = rel_bias.astype(F32)[:, jnp.clip(m, -MAX_REL, MAX_REL) + MAX_REL]
    skew = jnp.tile(ext, (1, PAIR))[:, :PAIR * width].reshape(ATT_HEADS, PAIR, width)
    bias = skew[:, :, PAIR - 1:PAIR - 1 + PAIR_BAND]
    first = (jnp.arange(PAIR) // CHUNK * CHUNK)[:, None]
    kk = jnp.arange(PAIR_BAND)[None, :]
    bias = jnp.where(((kk >= first) & (kk < first + BAND))[None], bias * LOG2E, -1e30)
    bias_t = jnp.transpose(bias, (0, 2, 1)).reshape(ATT_HEADS // 2, 2, PAIR_BAND, PAIR)
    return jnp.transpose(bias_t, (0, 2, 1, 3)).reshape(ATT_HEADS // 2, PAIR_BAND, 2 * PAIR)


def kernel(x, norm_mix_g, norm_ffn_g, final_norm_g, ab_w_in, pool_w, pool_scale, att_rel_bias,
           ab_w_out, sgu_w_in, sgu_b_in, sgu_ln_g, sgu_ln_b, sgu_w_s, sgu_b_s, sgu_w_out,
           moe_wg_router, moe_bg_router, moe_we_router, moe_be_router,
           moe_w_gate, moe_w_up, moe_w_down):
    b, s, d = x.shape
    row = lambda a: a.reshape(1, -1).astype(F32)

    def moe(h, layer, final_norm):
        wr, br = _router_params(moe_wg_router[layer], moe_bg_router[layer],
                                moe_we_router[layer], moe_be_router[layer])
        wd = moe_w_down[layer].astype(BF16).reshape(N_GROUPS, EXPERTS_PER_GROUP * D_EXPERT, d)
        return _moe(h, row(norm_ffn_g[layer]), wr, br, moe_w_gate[layer].astype(BF16),
                    moe_w_up[layer].astype(BF16), wd, row(final_norm_g), final_norm)

    w_in = ab_w_in[0]
    n_pqk = POOL_WIDTH + 2 * ATT_WIDTH
    p, q, k, vt = _inproj(x, row(norm_mix_g[0]), w_in[:, :n_pqk].astype(BF16),
                          w_in[:, n_pqk:].T.astype(BF16))
    h = _mix(x, p, q, k, vt, _band_bias_t(att_rel_bias[0]), pool_w[0].astype(BF16),
             row(pool_scale[0]), ab_w_out[0].astype(BF16))
    h = moe(h.reshape(b * s, d), 0, False)

    h = _sgu(h, row(norm_mix_g[1]), sgu_w_in[0].astype(BF16), row(sgu_b_in[0]),
             row(sgu_ln_g[0]), row(sgu_ln_b[0]), sgu_w_s[0].astype(BF16),
             sgu_b_s[0].astype(F32)[:, :, None], sgu_w_out[0].astype(BF16))
    h = moe(h, 1, True)
    return h.reshape(b, s, d)
```

```python
import functools

import jax
import jax.numpy as jnp
from jax import lax
from jax.experimental import pallas as pl
from jax.experimental.pallas import tpu as pltpu

EPS = 1e-6
CHUNK = 64
POOL_WINDOWS = (2, 4, 8, 16)
POOL_GROUP_DIM = 128
POOL_WIDTH = 512
POOL_HALO = 16
ATT_HEADS = 8
ATT_HEAD_DIM = 64
ATT_WIDTH = 512
LEFT_CHUNKS = 8
LEFT = LEFT_CHUNKS * CHUNK
BAND = LEFT + CHUNK
PAIR = 2 * CHUNK
PAIR_BAND = LEFT + PAIR
MAX_REL = 128
ATT_SCALE = ATT_HEAD_DIM ** -0.5
LOG2E = 1.4426950408889634
VT_ONES = 16
VT_PAIR = 2 * ATT_HEAD_DIM + VT_ONES
VT_ROWS = (ATT_HEADS // 2) * VT_PAIR
SGU_CHUNK = 128
SGU_HEADS = 8
N_GROUPS = 4
EXPERTS_PER_GROUP = 4
N_EXPERTS = 16
D_EXPERT = 256
ROUTER_LANES = 128
MOE_TILE = 1024
MOE_BLOCK = 320
PERM_BLOCK = 256
ROW_ALIGN = 16

VMEM_LIMIT = 56 * 1024 * 1024

F32 = jnp.float32
BF16 = jnp.bfloat16


def _rms(x, g):
    return x * lax.rsqrt(jnp.mean(x * x, axis=-1, keepdims=True) + EPS) * g


def _inproj_kernel(x_ref, g_ref, w_ref, wvt_ref, p_ref, q_ref, k_ref, vt_ref):
    i = pl.program_id(1)

    @pl.when(i == 0)
    def _():
        k_ref[...] = jnp.zeros_like(k_ref)
        vt_ref[...] = jnp.zeros_like(vt_ref)

    @pl.when(i > 0)
    def _():
        xn = _rms(x_ref[0], g_ref[...]).astype(BF16)
        z = jnp.dot(xn, w_ref[...], preferred_element_type=F32)
        p_ref[0] = z[:, :POOL_WIDTH]
        q_ref[0] = (z[:, POOL_WIDTH:POOL_WIDTH + ATT_WIDTH] * (ATT_SCALE * LOG2E)).astype(BF16)
        k_ref[0] = z[:, POOL_WIDTH + ATT_WIDTH:].astype(BF16)
        vt = lax.dot_general(wvt_ref[...], xn, (((1,), (1,)), ((), ())), preferred_element_type=F32)
        for j in range(ATT_HEADS // 2):
            rows = slice(2 * j * ATT_HEAD_DIM, 2 * (j + 1) * ATT_HEAD_DIM)
            vt_ref[0, j * VT_PAIR:j * VT_PAIR + 2 * ATT_HEAD_DIM, :] = vt[rows].astype(BF16)
            vt_ref[0, j * VT_PAIR + 2 * ATT_HEAD_DIM:(j + 1) * VT_PAIR, :] = jnp.ones(
                (VT_ONES, vt.shape[1]), BF16)


def _inproj(x, g, w_pqk, w_vt):
    b, s, d = x.shape
    tm = LEFT
    nt = s // tm
    prev = lambda bi, i: (bi, jnp.maximum(i - 1, 0), 0)
    const2 = lambda bi, i: (0, 0)
    return pl.pallas_call(
        _inproj_kernel,
        grid=(b, nt + 1),
        in_specs=[
            pl.BlockSpec((1, tm, d), prev),
            pl.BlockSpec((1, d), const2),
            pl.BlockSpec(w_pqk.shape, const2),
            pl.BlockSpec(w_vt.shape, const2),
        ],
        out_specs=[
            pl.BlockSpec((1, tm, POOL_WIDTH), prev),
            pl.BlockSpec((1, tm, ATT_WIDTH), prev),
            pl.BlockSpec((1, tm, ATT_WIDTH), lambda bi, i: (bi, i, 0)),
            pl.BlockSpec((1, VT_ROWS, tm), lambda bi, i: (bi, 0, i)),
        ],
        out_shape=[
            jax.ShapeDtypeStruct((b, s, POOL_WIDTH), F32),
            jax.ShapeDtypeStruct((b, s, ATT_WIDTH), BF16),
            jax.ShapeDtypeStruct((b, s + LEFT, ATT_WIDTH), BF16),
            jax.ShapeDtypeStruct((b, VT_ROWS, s + LEFT), BF16),
        ],
        compiler_params=pltpu.CompilerParams(
            dimension_semantics=("arbitrary", "arbitrary"), vmem_limit_bytes=VMEM_LIMIT),
        name="inproj",
    )(x, g, w_pqk, w_vt)


def _attend_tile(q_ref, k_ref, vt_ref, biast_ref, mix, st_buf, i, tq, masked, rows_done):
    lane = lax.broadcasted_iota(jnp.int32, (PAIR, 2 * ATT_HEAD_DIM), 1)
    even = lane < ATT_HEAD_DIM
    zero = jnp.zeros((), BF16)
    units = [(cp, j) for cp in range(tq // PAIR) for j in range(ATT_HEADS // 2)]

    def band_base(cp):
        return pl.multiple_of((i * (tq // PAIR) + cp) * PAIR, PAIR)

    def scores(u):
        cp, j = units[u]
        cols = slice(2 * j * ATT_HEAD_DIM, 2 * (j + 1) * ATT_HEAD_DIM)
        q2 = q_ref[0, cp * PAIR:(cp + 1) * PAIR, cols]
        qs = jnp.concatenate([jnp.where(even, q2, zero), jnp.where(even, zero, q2)], axis=0)
        kb = k_ref[0, pl.ds(band_base(cp), PAIR_BAND), cols]
        st_buf[u % 2] = lax.dot_general(kb, qs, (((1,), (1,)), ((), ())),
                                        preferred_element_type=F32)

    scores(0)
    for u, (cp, j) in enumerate(units):
        if u + 1 < len(units):
            scores(u + 1)
        cols = slice(2 * j * ATT_HEAD_DIM, 2 * (j + 1) * ATT_HEAD_DIM)
        base = band_base(cp)
        st = st_buf[u % 2] + biast_ref[j]
        if masked:
            valid = (lax.broadcasted_iota(jnp.int32, (PAIR_BAND, 1), 0) + base - LEFT) >= 0
            st = jnp.where(valid, st, -1e30)
        m = jnp.max(st, axis=0, keepdims=True)
        e = jnp.exp2(st - m).astype(BF16)
        vtb = vt_ref[0, j * VT_PAIR:(j + 1) * VT_PAIR, pl.ds(base, PAIR_BAND)]
        ot = jnp.dot(vtb, e, preferred_element_type=F32)
        ot = ot[:2 * ATT_HEAD_DIM] * (1.0 / ot[2 * ATT_HEAD_DIM:2 * ATT_HEAD_DIM + 1])
        o2t = jnp.concatenate([ot[:ATT_HEAD_DIM, :PAIR], ot[ATT_HEAD_DIM:, PAIR:]], axis=0)
        mix[cp * PAIR:(cp + 1) * PAIR, POOL_WIDTH + 2 * j * ATT_HEAD_DIM:
            POOL_WIDTH + 2 * (j + 1) * ATT_HEAD_DIM] = o2t.T.astype(BF16)
        if j == ATT_HEADS // 2 - 1:
            rows_done(cp)


def _mix_kernel(x_ref, p_ref, halo_ref, q_ref, k_ref, vt_ref, biast_ref, wpool_ref, pscale_ref,
                wout_ref, o_ref, pbuf, mix, st_buf, *, tq):
    i = pl.program_id(1)

    def pool():
        halo = halo_ref[0]
        pbuf[0:POOL_HALO, :] = jnp.where(i == 0, jnp.zeros_like(halo), halo)
        pbuf[POOL_HALO:, :] = p_ref[0]
        t_abs = i * tq + lax.broadcasted_iota(jnp.int32, (tq, 1), 0)
        for gi, w in enumerate(POOL_WINDOWS):
            cols = slice(gi * POOL_GROUP_DIM, (gi + 1) * POOL_GROUP_DIM)
            cur = pbuf[POOL_HALO:, cols]
            acc = cur
            for j in range(1, w):
                acc = acc + pbuf[POOL_HALO - j:POOL_HALO - j + tq, cols]
            cnt = jnp.minimum(t_abs + 1, w).astype(F32)
            pooled = (acc / cnt - cur).astype(BF16)
            y = jnp.dot(pooled, wpool_ref[gi], preferred_element_type=F32) * pscale_ref[:, cols]
            mix[:, cols] = y.astype(BF16)

    def out_proj(cp):
        rows = slice(cp * PAIR, (cp + 1) * PAIR)
        o_ref[0, rows, :] = x_ref[0, rows, :] + jnp.dot(mix[rows, :], wout_ref[...],
                                                      preferred_element_type=F32)

    def step(masked):
        pool()
        _attend_tile(q_ref, k_ref, vt_ref, biast_ref, mix, st_buf, i, tq, masked, out_proj)

    pl.when(i * tq < LEFT)(lambda: step(True))
    pl.when(i * tq >= LEFT)(lambda: step(False))


def _mix(x, p, q, k, vt, biast, w_pool, pool_scale, w_out):
    b, s, d = x.shape
    tq = 256
    halo_blocks = tq // POOL_HALO
    tile = lambda bi, i: (bi, i, 0)
    full = lambda bi, i: (bi, 0, 0)
    const2 = lambda bi, i: (0, 0)
    const3 = lambda bi, i: (0, 0, 0)
    return pl.pallas_call(
        functools.partial(_mix_kernel, tq=tq),
        grid=(b, s // tq),
        in_specs=[
            pl.BlockSpec((1, tq, d), tile),
            pl.BlockSpec((1, tq, POOL_WIDTH), tile),
            pl.BlockSpec((1, POOL_HALO, POOL_WIDTH),
                         lambda bi, i: (bi, jnp.maximum(i * halo_blocks - 1, 0), 0)),
            pl.BlockSpec((1, tq, ATT_WIDTH), tile),
            pl.BlockSpec((1, s + LEFT, ATT_WIDTH), full),
            pl.BlockSpec((1, VT_ROWS, s + LEFT), full),
            pl.BlockSpec(biast.shape, const3),
            pl.BlockSpec(w_pool.shape, const3),
            pl.BlockSpec(pool_scale.shape, const2),
            pl.BlockSpec(w_out.shape, const2),
        ],
        out_specs=pl.BlockSpec((1, tq, d), tile),
        out_shape=jax.ShapeDtypeStruct((b, s, d), F32),
        scratch_shapes=[
            pltpu.VMEM((POOL_HALO + tq, POOL_WIDTH), F32),
            pltpu.VMEM((tq, POOL_WIDTH + ATT_WIDTH), BF16),
            pltpu.VMEM((2, PAIR_BAND, 2 * PAIR), F32),
        ],
        compiler_params=pltpu.CompilerParams(
            dimension_semantics=("arbitrary", "arbitrary"), vmem_limit_bytes=VMEM_LIMIT),
        name="pool_attn_mix",
    )(x, p, p, q, k, vt, biast, w_pool, pool_scale, w_out)


def _sgu_kernel(h_ref, g_ref, win_ref, bin_ref, lng_ref, lnb_ref, ws_ref, bs_ref, wout_ref,
                o_ref, xn_ref, z_buf, gated, *, tm, sub):
    d = h_ref.shape[-1]
    hd = d // SGU_HEADS
    xn_ref[...] = _rms(h_ref[...], g_ref[...]).astype(BF16)
    row = lax.broadcasted_iota(jnp.int32, (SGU_CHUNK, SGU_CHUNK), 0)
    col = lax.broadcasted_iota(jnp.int32, (SGU_CHUNK, SGU_CHUNK), 1)
    ws = [jnp.where(row >= col, ws_ref[hh], jnp.zeros((), BF16)) for hh in range(SGU_HEADS)]

    def in_proj(s):
        z_buf[s % 2] = jnp.dot(xn_ref[s * sub:(s + 1) * sub, :], win_ref[...],
                               preferred_element_type=F32)

    in_proj(0)
    for s in range(tm // sub):
        if s + 1 < tm // sub:
            in_proj(s + 1)
        z = z_buf[s % 2] + bin_ref[...]
        z = 0.5 * z * (1.0 + lax.erf(z * (2.0 ** -0.5)))
        u = z[:, :d]
        v = z[:, d:]
        mu = jnp.mean(v, axis=-1, keepdims=True)
        vc = v - mu
        var = jnp.mean(vc * vc, axis=-1, keepdims=True)
        vn = (vc * lax.rsqrt(var + EPS) * lng_ref[...] + lnb_ref[...]).astype(BF16)
        for hh in range(SGU_HEADS):
            cols = slice(hh * hd, (hh + 1) * hd)
            for n in range(sub // SGU_CHUNK):
                rows = slice(n * SGU_CHUNK, (n + 1) * SGU_CHUNK)
                mixed = jnp.dot(ws[hh], vn[rows, cols], preferred_element_type=F32) + bs_ref[hh]
                gated[s * sub + n * SGU_CHUNK:s * sub + (n + 1) * SGU_CHUNK, cols] = (
                    u[rows, cols] * mixed).astype(BF16)
        blk = slice(s * sub, (s + 1) * sub)
        o_ref[blk, :] = h_ref[blk, :] + jnp.dot(gated[blk, :], wout_ref[...],
                                                preferred_element_type=F32)


def _sgu(h, g, w_in, b_in, ln_g, ln_b, w_s, b_s, w_out):
    n, d = h.shape
    tm, sub = 1024, 256
    tile = lambda i: (i, 0)
    const2 = lambda i: (0, 0)
    const3 = lambda i: (0, 0, 0)
    return pl.pallas_call(
        functools.partial(_sgu_kernel, tm=tm, sub=sub),
        grid=(n // tm,),
        in_specs=[
            pl.BlockSpec((tm, d), tile),
            pl.BlockSpec((1, d), const2),
            pl.BlockSpec(w_in.shape, const2),
            pl.BlockSpec(b_in.shape, const2),
            pl.BlockSpec(ln_g.shape, const2),
            pl.BlockSpec(ln_b.shape, const2),
            pl.BlockSpec(w_s.shape, const3),
            pl.BlockSpec(b_s.shape, const3),
            pl.BlockSpec(w_out.shape, const2),
        ],
        out_specs=pl.BlockSpec((tm, d), tile),
        out_shape=jax.ShapeDtypeStruct((n, d), F32),
        scratch_shapes=[
            pltpu.VMEM((tm, d), BF16),
            pltpu.VMEM((2, sub, 2 * d), F32),
            pltpu.VMEM((tm, d), BF16),
        ],
        compiler_params=pltpu.CompilerParams(
            dimension_semantics=("arbitrary",), vmem_limit_bytes=VMEM_LIMIT),
        name="sgu",
    )(h, g, w_in, b_in, ln_g, ln_b, w_s, b_s, w_out)


def _route(logits):
    lane = lax.broadcasted_iota(jnp.int32, logits.shape, 1)
    big = jnp.int32(ROUTER_LANES)
    neg = jnp.float32(-jnp.inf)
    gl = jnp.where(lane < N_GROUPS, logits, neg)
    gmax = jnp.max(gl, axis=-1, keepdims=True)
    gidx = jnp.min(jnp.where(gl == gmax, lane, big), axis=-1, keepdims=True)
    g_w = 1.0 / jnp.sum(jnp.exp(gl - gmax), axis=-1, keepdims=True)
    lo = N_GROUPS + EXPERTS_PER_GROUP * gidx
    el = jnp.where((lane >= lo) & (lane < lo + EXPERTS_PER_GROUP), logits, neg)
    m1 = jnp.max(el, axis=-1, keepdims=True)
    i1 = jnp.min(jnp.where(el == m1, lane, big), axis=-1, keepdims=True)
    el2 = jnp.where(lane == i1, neg, el)
    m2 = jnp.max(el2, axis=-1, keepdims=True)
    i2 = jnp.min(jnp.where(el2 == m2, lane, big), axis=-1, keepdims=True)
    e2 = jnp.exp(m2 - m1)
    den = 1.0 + e2
    w1 = g_w / den
    w2 = g_w * e2 / den
    return gidx, jnp.where(lane == i1 - lo, w1, 0.0) + jnp.where(lane == i2 - lo, w2, 0.0)


def _moe_kernel(h_ref, g_ref, wr_ref, br_ref, tri_ref, wg_ref, wu_ref, wd_ref, gf_ref,
                o_ref, xs_ref, ws_ref, ys_ref, pos_ref, hg_buf, hu_buf, act_ref, bounds, *, final_norm):
    grp = pl.program_id(1)
    tm, d = h_ref.shape
    nperm = tm // PERM_BLOCK
    lane = lax.broadcasted_iota(jnp.int32, (tm, ROUTER_LANES), 1)

    @pl.when(grp == 0)
    def _():
        xn = _rms(h_ref[...], g_ref[...])
        x_hi = xn.astype(BF16)
        x_lo = (xn - x_hi.astype(F32)).astype(BF16)
        both = jnp.dot(x_hi, wr_ref[...], preferred_element_type=F32)
        logits = (both[:, :ROUTER_LANES] + both[:, ROUTER_LANES:]
                  + jnp.dot(x_lo, wr_ref[:, :ROUTER_LANES], preferred_element_type=F32)) + br_ref[...]
        gidx, cw = _route(logits)

        onehot = jnp.where(lane == gidx, 1.0, 0.0)
        rank = jnp.dot(tri_ref[...], onehot.astype(BF16), preferred_element_type=F32)
        counts = jnp.sum(onehot, axis=0, keepdims=True)
        lane1 = lane[:1]
        start = jnp.zeros_like(counts)
        for g in range(N_GROUPS - 1):
            c_g = jnp.sum(jnp.where(lane1 == g, counts, 0.0), axis=-1, keepdims=True)
            start = start + jnp.where(lane1 > g, c_g, 0.0)
        for g in range(N_GROUPS):
            bounds[g] = start[0, g].astype(jnp.int32)
        bounds[N_GROUPS] = jnp.int32(tm)
        pos = jnp.sum(jnp.where(lane == gidx, rank + start, 0.0), axis=-1, keepdims=True)
        pos_b = jnp.broadcast_to(pos, (tm, ROUTER_LANES))
        pos_ref[...] = pos_b
        pos_row = pos_b.T[:1, :].astype(jnp.int32)

        w_hi = cw.astype(BF16).astype(F32)
        w_mid = (cw - w_hi).astype(BF16).astype(F32)
        w_lo = (cw - w_hi - w_mid).astype(BF16).astype(F32)
        w3 = (w_hi + pltpu.roll(w_mid, EXPERTS_PER_GROUP, 1)
              + pltpu.roll(w_lo, 2 * EXPERTS_PER_GROUP, 1)).astype(BF16)

        for rb in range(nperm):
            rows = slice(rb * PERM_BLOCK, (rb + 1) * PERM_BLOCK)
            sorted_row = lax.broadcasted_iota(jnp.int32, (PERM_BLOCK, tm), 0) + rb * PERM_BLOCK
            perm = jnp.where(sorted_row == pos_row, 1.0, 0.0).astype(BF16)
            xs_ref[rows, :] = jnp.dot(perm, x_hi, preferred_element_type=F32).astype(BF16)
            ws_ref[rows, :] = jnp.dot(perm, w3, preferred_element_type=F32)
        ys_ref[...] = jnp.zeros_like(ys_ref)

    seg_lo = bounds[grp]
    seg_hi = bounds[grp + 1]
    first = (seg_lo // ROW_ALIGN) * ROW_ALIGN
    lane_b = lax.broadcasted_iota(jnp.int32, (MOE_BLOCK, ROUTER_LANES), 1)
    for k in range(pl.cdiv(tm, MOE_BLOCK)):
        blk_lo = first + k * MOE_BLOCK

        @pl.when(blk_lo < seg_hi)
        def _(blk_lo=blk_lo):
            row0 = pl.multiple_of(jnp.minimum(blk_lo, tm - MOE_BLOCK), ROW_ALIGN)
            rows = pl.ds(row0, MOE_BLOCK)
            sorted_row = lax.broadcasted_iota(jnp.int32, (MOE_BLOCK, 1), 0) + row0
            in_seg = ((sorted_row >= jnp.maximum(seg_lo, blk_lo))
                      & (sorted_row < jnp.minimum(seg_hi, blk_lo + MOE_BLOCK)))
            w3 = ws_ref[rows, :]

            def gate_up(j):
                hg_buf[j % 2] = jnp.dot(xs_ref[rows, :], wg_ref[j], preferred_element_type=F32)
                hu_buf[j % 2] = jnp.dot(xs_ref[rows, :], wu_ref[j], preferred_element_type=F32)

            gate_up(0)
            for j in range(EXPERTS_PER_GROUP):
                if j + 1 < EXPERTS_PER_GROUP:
                    gate_up(j + 1)
                mine = (lane_b % EXPERTS_PER_GROUP == j) & (lane_b < 3 * EXPERTS_PER_GROUP)
                c = jnp.sum(jnp.where(mine, w3, 0.0), axis=-1, keepdims=True)
                c = jnp.where(in_seg, c, 0.0)
                hg = hg_buf[j % 2]
                act = hg * (1.0 / (1.0 + jnp.exp(-hg))) * hu_buf[j % 2] * c
                act_ref[:, j * D_EXPERT:(j + 1) * D_EXPERT] = act.astype(BF16)
            ys_ref[rows, :] += jnp.dot(act_ref[...], wd_ref[0], preferred_element_type=F32)

    @pl.when(grp == N_GROUPS - 1)
    def _():
        xs_ref[...] = ys_ref[...].astype(BF16)
        for tb in range(nperm):
            rows = slice(tb * PERM_BLOCK, (tb + 1) * PERM_BLOCK)
            pos_i = pos_ref[rows, :1].astype(jnp.int32)
            perm_t = jnp.where(lax.broadcasted_iota(jnp.int32, (PERM_BLOCK, tm), 1) == pos_i,
                               1.0, 0.0).astype(BF16)
            out = h_ref[rows, :] + jnp.dot(perm_t, xs_ref[...], preferred_element_type=F32)
            if final_norm:
                out = _rms(out, gf_ref[...])
            o_ref[rows, :] = out


def _moe(h, g, wr, br, wg, wu, wd, gf, final_norm):
    n, d = h.shape
    tm = MOE_TILE
    tile = lambda i, e: (i, 0)
    const2 = lambda i, e: (0, 0)
    group = lambda i, e: (e, 0, 0)
    tri = jnp.tri(tm, tm, -1, dtype=BF16)
    return pl.pallas_call(
        functools.partial(_moe_kernel, final_norm=final_norm),
        grid=(n // tm, N_GROUPS),
        in_specs=[
            pl.BlockSpec((tm, d), tile),
            pl.BlockSpec((1, d), const2),
            pl.BlockSpec(wr.shape, const2),
            pl.BlockSpec(br.shape, const2),
            pl.BlockSpec(tri.shape, const2),
            pl.BlockSpec((EXPERTS_PER_GROUP,) + wg.shape[1:], group),
            pl.BlockSpec((EXPERTS_PER_GROUP,) + wu.shape[1:], group),
            pl.BlockSpec((1,) + wd.shape[1:], group),
            pl.BlockSpec((1, d), const2),
        ],
        out_specs=pl.BlockSpec((tm, d), tile),
        out_shape=jax.ShapeDtypeStruct((n, d), F32),
        scratch_shapes=[
            pltpu.VMEM((tm, d), BF16),
            pltpu.VMEM((tm, ROUTER_LANES), F32),
            pltpu.VMEM((tm, d), F32),
            pltpu.VMEM((tm, ROUTER_LANES), F32),
            pltpu.VMEM((2, MOE_BLOCK, D_EXPERT), F32),
            pltpu.VMEM((2, MOE_BLOCK, D_EXPERT), F32),
            pltpu.VMEM((MOE_BLOCK, EXPERTS_PER_GROUP * D_EXPERT), BF16),
            pltpu.SMEM((8,), jnp.int32),
        ],
        compiler_params=pltpu.CompilerParams(
            dimension_semantics=("arbitrary", "arbitrary"), vmem_limit_bytes=VMEM_LIMIT),
        name="hier_moe",
    )(h, g, wr, br, tri, wg, wu, wd, gf)


def _router_params(wg_router, bg_router, we_router, be_router):
    d = wg_router.shape[0]
    we = jnp.transpose(we_router, (1, 0, 2)).reshape(d, N_EXPERTS)
    w = jnp.concatenate([wg_router, we], axis=1)
    w = jnp.pad(w, ((0, 0), (0, ROUTER_LANES - w.shape[1])))
    bias = jnp.concatenate([bg_router, be_router.reshape(N_EXPERTS)])
    bias = jnp.pad(bias, (0, ROUTER_LANES - bias.shape[0])).reshape(1, ROUTER_LANES)
    w_hi = w.astype(BF16)
    w_lo = (w - w_hi.astype(F32)).astype(BF16)
    return jnp.concatenate([w_hi, w_lo], axis=1), bias.astype(F32)


def _band_bias_t(rel_bias):
    width = PAIR_BAND + PAIR - 1
    m = jnp.arange(width + 1) - (PAIR - 1) - LEFT
    ext = rel_bias.astype(F32)[:, jnp.clip(m, -MAX_REL, MAX_REL) + MAX_REL]
    skew = jnp.tile(ext, (1, PAIR))[:, :PAIR * width].reshape(ATT_HEADS, PAIR, width)
    bias = skew[:, :, PAIR - 1:PAIR - 1 + PAIR_BAND]
    first = (jnp.arange(PAIR) // CHUNK * CHUNK)[:, None]
    kk = jnp.arange(PAIR_BAND)[None, :]
    bias = jnp.where(((kk >= first) & (kk < first + BAND))[None], bias * LOG2E, -1e30)
    bias_t = jnp.transpose(bias, (0, 2, 1)).reshape(ATT_HEADS // 2, 2, PAIR_BAND, PAIR)
    return jnp.transpose(bias_t, (0, 2, 1, 3)).reshape(ATT_HEADS // 2, PAIR_BAND, 2 * PAIR)


def kernel(x, norm_mix_g, norm_ffn_g, final_norm_g, ab_w_in, pool_w, pool_scale, att_rel_bias,
           ab_w_out, sgu_w_in, sgu_b_in, sgu_ln_g, sgu_ln_b, sgu_w_s, sgu_b_s, sgu_w_out,
           moe_wg_router, moe_bg_router, moe_we_router, moe_be_router,
           moe_w_gate, moe_w_up, moe_w_down):
    b, s, d = x.shape
    row = lambda a: a.reshape(1, -1).astype(F32)

    def moe(h, layer, final_norm):
        wr, br = _router_params(moe_wg_router[layer], moe_bg_router[layer],
                                moe_we_router[layer], moe_be_router[layer])
        wd = moe_w_down[layer].astype(BF16).reshape(N_GROUPS, EXPERTS_PER_GROUP * D_EXPERT, d)
        return _moe(h, row(norm_ffn_g[layer]), wr, br, moe_w_gate[layer].astype(BF16),
                    moe_w_up[layer].astype(BF16), wd, row(final_norm_g), final_norm)

    w_in = ab_w_in[0]
    n_pqk = POOL_WIDTH + 2 * ATT_WIDTH
    p, q, k, vt = _inproj(x, row(norm_mix_g[0]), w_in[:, :n_pqk].astype(BF16),
                          w_in[:, n_pqk:].T.astype(BF16))
    h = _mix(x, p, q, k, vt, _band_bias_t(att_rel_bias[0]), pool_w[0].astype(BF16),
             row(pool_scale[0]), ab_w_out[0].astype(BF16))
    h = moe(h.reshape(b * s, d), 0, False)

    h = _sgu(h, row(norm_mix_g[1]), sgu_w_in[0].astype(BF16), row(sgu_b_in[0]),
             row(sgu_ln_g[0]), row(sgu_ln_b[0]), sgu_w_s[0].astype(BF16),
             sgu_b_s[0].astype(F32)[:, :, None], sgu_w_out[0].astype(BF16))
    h = moe(h, 1, True)
    return h.reshape(b, s, d)
```

```python
import functools

import jax
import jax.numpy as jnp
from jax import lax
from jax.experimental import pallas as pl
from jax.experimental.pallas import tpu as pltpu

EPS = 1e-6
CHUNK = 64
POOL_WINDOWS = (2, 4, 8, 16)
POOL_GROUP_DIM = 128
POOL_WIDTH = 512
POOL_HALO = 16
ATT_HEADS = 8
ATT_HEAD_DIM = 64
ATT_WIDTH = 512
LEFT_CHUNKS = 8
LEFT = LEFT_CHUNKS * CHUNK
BAND = LEFT + CHUNK
PAIR = 2 * CHUNK
PAIR_BAND = LEFT + PAIR
MAX_REL = 128
ATT_SCALE = ATT_HEAD_DIM ** -0.5
LOG2E = 1.4426950408889634
VT_ONES = 16
VT_PAIR = 2 * ATT_HEAD_DIM + VT_ONES
VT_ROWS = (ATT_HEADS // 2) * VT_PAIR
SGU_CHUNK = 128
SGU_HEADS = 8
N_GROUPS = 4
EXPERTS_PER_GROUP = 4
N_EXPERTS = 16
D_EXPERT = 256
ROUTER_LANES = 128
MOE_TILE = 1024
MOE_BLOCK = 320
PERM_BLOCK = 256
ROW_ALIGN = 16

VMEM_LIMIT = 56 * 1024 * 1024

F32 = jnp.float32
BF16 = jnp.bfloat16


def _rms(x, g):
    return x * lax.rsqrt(jnp.mean(x * x, axis=-1, keepdims=True) + EPS) * g


def _inproj_kernel(x_ref, g_ref, w_ref, wvt_ref, p_ref, q_ref, k_ref, vt_ref):
    i = pl.program_id(1)

    @pl.when(i == 0)
    def _():
        k_ref[...] = jnp.zeros_like(k_ref)
        vt_ref[...] = jnp.zeros_like(vt_ref)

    @pl.when(i > 0)
    def _():
        xn = _rms(x_ref[0], g_ref[...]).astype(BF16)
        z = jnp.dot(xn, w_ref[...], preferred_element_type=F32)
        p_ref[0] = z[:, :POOL_WIDTH]
        q_ref[0] = (z[:, POOL_WIDTH:POOL_WIDTH + ATT_WIDTH] * (ATT_SCALE * LOG2E)).astype(BF16)
        k_ref[0] = z[:, POOL_WIDTH + ATT_WIDTH:].astype(BF16)
        vt = lax.dot_general(wvt_ref[...], xn, (((1,), (1,)), ((), ())), preferred_element_type=F32)
        for j in range(ATT_HEADS // 2):
            rows = slice(2 * j * ATT_HEAD_DIM, 2 * (j + 1) * ATT_HEAD_DIM)
            vt_ref[0, j * VT_PAIR:j * VT_PAIR + 2 * ATT_HEAD_DIM, :] = vt[rows].astype(BF16)
            vt_ref[0, j * VT_PAIR + 2 * ATT_HEAD_DIM:(j + 1) * VT_PAIR, :] = jnp.ones(
                (VT_ONES, vt.shape[1]), BF16)


def _inproj(x, g, w_pqk, w_vt):
    b, s, d = x.shape
    tm = LEFT
    nt = s // tm
    prev = lambda bi, i: (bi, jnp.maximum(i - 1, 0), 0)
    const2 = lambda bi, i: (0, 0)
    return pl.pallas_call(
        _inproj_kernel,
        grid=(b, nt + 1),
        in_specs=[
            pl.BlockSpec((1, tm, d), prev),
            pl.BlockSpec((1, d), const2),
            pl.BlockSpec(w_pqk.shape, const2),
            pl.BlockSpec(w_vt.shape, const2),
        ],
        out_specs=[
            pl.BlockSpec((1, tm, POOL_WIDTH), prev),
            pl.BlockSpec((1, tm, ATT_WIDTH), prev),
            pl.BlockSpec((1, tm, ATT_WIDTH), lambda bi, i: (bi, i, 0)),
            pl.BlockSpec((1, VT_ROWS, tm), lambda bi, i: (bi, 0, i)),
        ],
        out_shape=[
            jax.ShapeDtypeStruct((b, s, POOL_WIDTH), F32),
            jax.ShapeDtypeStruct((b, s, ATT_WIDTH), BF16),
            jax.ShapeDtypeStruct((b, s + LEFT, ATT_WIDTH), BF16),
            jax.ShapeDtypeStruct((b, VT_ROWS, s + LEFT), BF16),
        ],
        compiler_params=pltpu.CompilerParams(
            dimension_semantics=("arbitrary", "arbitrary"), vmem_limit_bytes=VMEM_LIMIT),
        name="inproj",
    )(x, g, w_pqk, w_vt)


def _attend_tile(q_ref, k_ref, vt_ref, biast_ref, mix, st_buf, i, tq, masked, rows_done):
    lane = lax.broadcasted_iota(jnp.int32, (PAIR, 2 * ATT_HEAD_DIM), 1)
    even = lane < ATT_HEAD_DIM
    zero = jnp.zeros((), BF16)
    units = [(cp, j) for cp in range(tq // PAIR) for j in range(ATT_HEADS // 2)]

    def band_base(cp):
        return pl.multiple_of((i * (tq // PAIR) + cp) * PAIR, PAIR)

    def scores(u):
        cp, j = units[u]
        cols = slice(2 * j * ATT_HEAD_DIM, 2 * (j + 1) * ATT_HEAD_DIM)
        q2 = q_ref[0, cp * PAIR:(cp + 1) * PAIR, cols]
        qs = jnp.concatenate([jnp.where(even, q2, zero), jnp.where(even, zero, q2)], axis=0)
        kb = k_ref[0, pl.ds(band_base(cp), PAIR_BAND), cols]
        st_buf[u % 2] = lax.dot_general(kb, qs, (((1,), (1,)), ((), ())),
                                        preferred_element_type=F32)

    scores(0)
    for u, (cp, j) in enumerate(units):
        if u + 1 < len(units):
            scores(u + 1)
        cols = slice(2 * j * ATT_HEAD_DIM, 2 * (j + 1) * ATT_HEAD_DIM)
        base = band_base(cp)
        st = st_buf[u % 2] + biast_ref[j]
        if masked:
            valid = (lax.broadcasted_iota(jnp.int32, (PAIR_BAND, 1), 0) + base - LEFT) >= 0
            st = jnp.where(valid, st, -1e30)
        m = jnp.max(st, axis=0, keepdims=True)
        e = jnp.exp2(st - m).astype(BF16)
        vtb = vt_ref[0, j * VT_PAIR:(j + 1) * VT_PAIR, pl.ds(base, PAIR_BAND)]
        ot = jnp.dot(vtb, e, preferred_element_type=F32)
        ot = ot[:2 * ATT_HEAD_DIM] * (1.0 / ot[2 * ATT_HEAD_DIM:2 * ATT_HEAD_DIM + 1])
        o2t = jnp.concatenate([ot[:ATT_HEAD_DIM, :PAIR], ot[ATT_HEAD_DIM:, PAIR:]], axis=0)
        mix[cp * PAIR:(cp + 1) * PAIR, POOL_WIDTH + 2 * j * ATT_HEAD_DIM:
            POOL_WIDTH + 2 * (j + 1) * ATT_HEAD_DIM] = o2t.T.astype(BF16)
        if j == ATT_HEADS // 2 - 1:
            rows_done(cp)


def _mix_kernel(x_ref, p_ref, halo_ref, q_ref, k_ref, vt_ref, biast_ref, wpool_ref, pscale_ref,
                wout_ref, o_ref, pbuf, mix, st_buf, *, tq):
    i = pl.program_id(1)

    def pool():
        halo = halo_ref[0]
        pbuf[0:POOL_HALO, :] = jnp.where(i == 0, jnp.zeros_like(halo), halo)
        pbuf[POOL_HALO:, :] = p_ref[0]
        t_abs = i * tq + lax.broadcasted_iota(jnp.int32, (tq, 1), 0)
        for gi, w in enumerate(POOL_WINDOWS):
            cols = slice(gi * POOL_GROUP_DIM, (gi + 1) * POOL_GROUP_DIM)
            cur = pbuf[POOL_HALO:, cols]
            acc = cur
            for j in range(1, w):
                acc = acc + pbuf[POOL_HALO - j:POOL_HALO - j + tq, cols]
            cnt = jnp.minimum(t_abs + 1, w).astype(F32)
            pooled = (acc / cnt - cur).astype(BF16)
            y = jnp.dot(pooled, wpool_ref[gi], preferred_element_type=F32) * pscale_ref[:, cols]
            mix[:, cols] = y.astype(BF16)

    def out_proj(cp):
        rows = slice(cp * PAIR, (cp + 1) * PAIR)
        o_ref[0, rows, :] = x_ref[0, rows, :] + jnp.dot(mix[rows, :], wout_ref[...],
                                                      preferred_element_type=F32)

    def step(masked):
        pool()
        _attend_tile(q_ref, k_ref, vt_ref, biast_ref, mix, st_buf, i, tq, masked, out_proj)

    pl.when(i * tq < LEFT)(lambda: step(True))
    pl.when(i * tq >= LEFT)(lambda: step(False))


def _mix(x, p, q, k, vt, biast, w_pool, pool_scale, w_out):
    b, s, d = x.shape
    tq = 256
    halo_blocks = tq // POOL_HALO
    tile = lambda bi, i: (bi, i, 0)
    full = lambda bi, i: (bi, 0, 0)
    const2 = lambda bi, i: (0, 0)
    const3 = lambda bi, i: (0, 0, 0)
    return pl.pallas_call(
        functools.partial(_mix_kernel, tq=tq),
        grid=(b, s // tq),
        in_specs=[
            pl.BlockSpec((1, tq, d), tile),
            pl.BlockSpec((1, tq, POOL_WIDTH), tile),
            pl.BlockSpec((1, POOL_HALO, POOL_WIDTH),
                         lambda bi, i: (bi, jnp.maximum(i * halo_blocks - 1, 0), 0)),
            pl.BlockSpec((1, tq, ATT_WIDTH), tile),
            pl.BlockSpec((1, s + LEFT, ATT_WIDTH), full),
            pl.BlockSpec((1, VT_ROWS, s + LEFT), full),
            pl.BlockSpec(biast.shape, const3),
            pl.BlockSpec(w_pool.shape, const3),
            pl.BlockSpec(pool_scale.shape, const2),
            pl.BlockSpec(w_out.shape, const2),
        ],
        out_specs=pl.BlockSpec((1, tq, d), tile),
        out_shape=jax.ShapeDtypeStruct((b, s, d), F32),
        scratch_shapes=[
            pltpu.VMEM((POOL_HALO + tq, POOL_WIDTH), F32),
            pltpu.VMEM((tq, POOL_WIDTH + ATT_WIDTH), BF16),
            pltpu.VMEM((2, PAIR_BAND, 2 * PAIR), F32),
        ],
        compiler_params=pltpu.CompilerParams(
            dimension_semantics=("arbitrary", "arbitrary"), vmem_limit_bytes=VMEM_LIMIT),
        name="pool_attn_mix",
    )(x, p, p, q, k, vt, biast, w_pool, pool_scale, w_out)


def _sgu_kernel(h_ref, g_ref, win_ref, bin_ref, lng_ref, lnb_ref, ws_ref, bs_ref, wout_ref,
                o_ref, xn_ref, z_buf, gated, *, tm, sub):
    d = h_ref.shape[-1]
    hd = d // SGU_HEADS
    xn_ref[...] = _rms(h_ref[...], g_ref[...]).astype(BF16)
    row = lax.broadcasted_iota(jnp.int32, (SGU_CHUNK, SGU_CHUNK), 0)
    col = lax.broadcasted_iota(jnp.int32, (SGU_CHUNK, SGU_CHUNK), 1)
    ws = [jnp.where(row >= col, ws_ref[hh], jnp.zeros((), BF16)) for hh in range(SGU_HEADS)]

    def in_proj(s):
        z_buf[s % 2] = jnp.dot(xn_ref[s * sub:(s + 1) * sub, :], win_ref[...],
                               preferred_element_type=F32)

    in_proj(0)
    for s in range(tm // sub):
        if s + 1 < tm // sub:
            in_proj(s + 1)
        z = z_buf[s % 2] + bin_ref[...]
        z = 0.5 * z * (1.0 + lax.erf(z * (2.0 ** -0.5)))
        u = z[:, :d]
        v = z[:, d:]
        mu = jnp.mean(v, axis=-1, keepdims=True)
        vc = v - mu
        var = jnp.mean(vc * vc, axis=-1, keepdims=True)
        vn = (vc * lax.rsqrt(var + EPS) * lng_ref[...] + lnb_ref[...]).astype(BF16)
        for hh in range(SGU_HEADS):
            cols = slice(hh * hd, (hh + 1) * hd)
            for n in range(sub // SGU_CHUNK):
                rows = slice(n * SGU_CHUNK, (n + 1) * SGU_CHUNK)
                mixed = jnp.dot(ws[hh], vn[rows, cols], preferred_element_type=F32) + bs_ref[hh]
                gated[s * sub + n * SGU_CHUNK:s * sub + (n + 1) * SGU_CHUNK, cols] = (
                    u[rows, cols] * mixed).astype(BF16)
        blk = slice(s * sub, (s + 1) * sub)
        o_ref[blk, :] = h_ref[blk, :] + jnp.dot(gated[blk, :], wout_ref[...],
                                                preferred_element_type=F32)


def _sgu(h, g, w_in, b_in, ln_g, ln_b, w_s, b_s, w_out):
    n, d = h.shape
    tm, sub = 1024, 256
    tile = lambda i: (i, 0)
    const2 = lambda i: (0, 0)
    const3 = lambda i: (0, 0, 0)
    return pl.pallas_call(
        functools.partial(_sgu_kernel, tm=tm, sub=sub),
        grid=(n // tm,),
        in_specs=[
            pl.BlockSpec((tm, d), tile),
            pl.BlockSpec((1, d), const2),
            pl.BlockSpec(w_in.shape, const2),
            pl.BlockSpec(b_in.shape, const2),
            pl.BlockSpec(ln_g.shape, const2),
            pl.BlockSpec(ln_b.shape, const2),
            pl.BlockSpec(w_s.shape, const3),
            pl.BlockSpec(b_s.shape, const3),
            pl.BlockSpec(w_out.shape, const2),
        ],
        out_specs=pl.BlockSpec((tm, d), tile),
        out_shape=jax.ShapeDtypeStruct((n, d), F32),
        scratch_shapes=[
            pltpu.VMEM((tm, d), BF16),
            pltpu.VMEM((2, sub, 2 * d), F32),
            pltpu.VMEM((tm, d), BF16),
        ],
        compiler_params=pltpu.CompilerParams(
            dimension_semantics=("arbitrary",), vmem_limit_bytes=VMEM_LIMIT),
        name="sgu",
    )(h, g, w_in, b_in, ln_g, ln_b, w_s, b_s, w_out)


def _route_rows(lt):
    neg = jnp.float32(-jnp.inf)

    def first_max(rows):
        m = functools.reduce(jnp.maximum, rows)
        idx = jnp.int32(len(rows))
        for r in reversed(range(len(rows))):
            idx = jnp.where(rows[r] == m, jnp.int32(r), idx)
        return m, idx

    gl = [lt[g:g + 1, :] for g in range(N_GROUPS)]
    gmax, gidx = first_max(gl)
    g_w = 1.0 / functools.reduce(jnp.add, [jnp.exp(x - gmax) for x in gl])
    el = []
    for j in range(EXPERTS_PER_GROUP):
        row = lt[N_GROUPS + j:N_GROUPS + j + 1, :]
        for g in range(1, N_GROUPS):
            r = N_GROUPS + EXPERTS_PER_GROUP * g + j
            row = jnp.where(gidx == g, lt[r:r + 1, :], row)
        el.append(row)
    m1, i1 = first_max(el)
    m2, i2 = first_max([jnp.where(i1 == j, neg, el[j]) for j in range(EXPERTS_PER_GROUP)])
    e2 = jnp.exp(m2 - m1)
    den = 1.0 + e2
    w1 = g_w / den
    w2 = g_w * e2 / den
    cw = [jnp.where(i1 == j, w1, 0.0) + jnp.where(i2 == j, w2, 0.0) for j in range(EXPERTS_PER_GROUP)]
    return gidx, cw


def _moe_kernel(h_ref, g_ref, wr_ref, br_ref, triu_ref, wg_ref, wu_ref, wd_ref, gf_ref,
                o_ref, xs_ref, ws_ref, ys_ref, pos_ref, hg_buf, hu_buf, act_ref, bounds, *, final_norm):
    grp = pl.program_id(1)
    tm, d = h_ref.shape
    nperm = tm // PERM_BLOCK

    @pl.when(grp == 0)
    def _():
        xn = _rms(h_ref[...], g_ref[...])
        x_hi = xn.astype(BF16)
        x_lo = (xn - x_hi.astype(F32)).astype(BF16)
        both = jnp.dot(x_hi, wr_ref[...], preferred_element_type=F32)
        logits = (both[:, :ROUTER_LANES] + both[:, ROUTER_LANES:]
                  + jnp.dot(x_lo, wr_ref[:, :ROUTER_LANES], preferred_element_type=F32)) + br_ref[...]
        gidx, cw = _route_rows(logits.T)

        row16 = lax.broadcasted_iota(jnp.int32, (ROW_ALIGN, tm), 0)
        onehot = jnp.where(row16 == gidx, 1.0, 0.0)
        rank = jnp.dot(onehot.astype(BF16), triu_ref[...], preferred_element_type=F32)
        pos_row = jnp.zeros((1, tm), F32)
        start = jnp.zeros((1, 1), F32)
        for g in range(N_GROUPS):
            bounds[g] = start[0, 0].astype(jnp.int32)
            mine = gidx == g
            pos_row = jnp.where(mine, rank[g:g + 1, :] + start, pos_row)
            start = start + jnp.sum(jnp.where(mine, 1.0, 0.0), axis=-1, keepdims=True)
        bounds[N_GROUPS] = jnp.int32(tm)

        packed = jnp.where(row16 == 0, pos_row, 0.0)
        for j in range(EXPERTS_PER_GROUP):
            piece = cw[j]
            for p in range(3):
                part = piece.astype(BF16).astype(F32)
                packed = packed + jnp.where(row16 == 1 + p * EXPERTS_PER_GROUP + j, part, 0.0)
                piece = piece - part
        packed = jnp.concatenate(
            [packed, jnp.zeros((ROUTER_LANES - ROW_ALIGN, tm), F32)], axis=0)
        tok = packed.T
        pos_ref[...] = tok
        w3 = tok.astype(BF16)
        pos_row = pos_row.astype(jnp.int32)

        for rb in range(nperm):
            rows = slice(rb * PERM_BLOCK, (rb + 1) * PERM_BLOCK)
            sorted_row = lax.broadcasted_iota(jnp.int32, (PERM_BLOCK, tm), 0) + rb * PERM_BLOCK
            perm = jnp.where(sorted_row == pos_row, 1.0, 0.0).astype(BF16)
            xs_ref[rows, :] = jnp.dot(perm, x_hi, preferred_element_type=F32).astype(BF16)
            ws_ref[rows, :] = jnp.dot(perm, w3, preferred_element_type=F32)
        ys_ref[...] = jnp.zeros_like(ys_ref)

    seg_lo = bounds[grp]
    seg_hi = bounds[grp + 1]
    first = (seg_lo // ROW_ALIGN) * ROW_ALIGN
    lane_b = lax.broadcasted_iota(jnp.int32, (MOE_BLOCK, ROUTER_LANES), 1)
    for k in range(pl.cdiv(tm, MOE_BLOCK)):
        blk_lo = first + k * MOE_BLOCK

        @pl.when(blk_lo < seg_hi)
        def _(blk_lo=blk_lo):
            row0 = pl.multiple_of(jnp.minimum(blk_lo, tm - MOE_BLOCK), ROW_ALIGN)
            rows = pl.ds(row0, MOE_BLOCK)
            sorted_row = lax.broadcasted_iota(jnp.int32, (MOE_BLOCK, 1), 0) + row0
            in_seg = ((sorted_row >= jnp.maximum(seg_lo, blk_lo))
                      & (sorted_row < jnp.minimum(seg_hi, blk_lo + MOE_BLOCK)))
            w3 = ws_ref[rows, :]

            def gate_up(j):
                hg_buf[j % 2] = jnp.dot(xs_ref[rows, :], wg_ref[j], preferred_element_type=F32)
                hu_buf[j % 2] = jnp.dot(xs_ref[rows, :], wu_ref[j], preferred_element_type=F32)

            gate_up(0)
            for j in range(EXPERTS_PER_GROUP):
                if j + 1 < EXPERTS_PER_GROUP:
                    gate_up(j + 1)
                mine = (((lane_b - 1) % EXPERTS_PER_GROUP == j) & (lane_b >= 1)
                        & (lane_b <= 3 * EXPERTS_PER_GROUP))
                c = jnp.sum(jnp.where(mine, w3, 0.0), axis=-1, keepdims=True)
                c = jnp.where(in_seg, c, 0.0)
                hg = hg_buf[j % 2]
                act = hg * (1.0 / (1.0 + jnp.exp(-hg))) * hu_buf[j % 2] * c
                act_ref[:, j * D_EXPERT:(j + 1) * D_EXPERT] = act.astype(BF16)
            ys_ref[rows, :] += jnp.dot(act_ref[...], wd_ref[0], preferred_element_type=F32)

    @pl.when(grp == N_GROUPS - 1)
    def _():
        xs_ref[...] = ys_ref[...].astype(BF16)
        for tb in range(nperm):
            rows = slice(tb * PERM_BLOCK, (tb + 1) * PERM_BLOCK)
            pos_i = pos_ref[rows, :1].astype(jnp.int32)
            perm_t = jnp.where(lax.broadcasted_iota(jnp.int32, (PERM_BLOCK, tm), 1) == pos_i,
                               1.0, 0.0).astype(BF16)
            out = h_ref[rows, :] + jnp.dot(perm_t, xs_ref[...], preferred_element_type=F32)
            if final_norm:
                out = _rms(out, gf_ref[...])
            o_ref[rows, :] = out


def _moe(h, g, wr, br, wg, wu, wd, gf, final_norm):
    n, d = h.shape
    tm = MOE_TILE
    tile = lambda i, e: (i, 0)
    const2 = lambda i, e: (0, 0)
    group = lambda i, e: (e, 0, 0)
    triu = jnp.tri(tm, tm, -1, dtype=BF16).T
    return pl.pallas_call(
        functools.partial(_moe_kernel, final_norm=final_norm),
        grid=(n // tm, N_GROUPS),
        in_specs=[
            pl.BlockSpec((tm, d), tile),
            pl.BlockSpec((1, d), const2),
            pl.BlockSpec(wr.shape, const2),
            pl.BlockSpec(br.shape, const2),
            pl.BlockSpec(triu.shape, const2),
            pl.BlockSpec((EXPERTS_PER_GROUP,) + wg.shape[1:], group),
            pl.BlockSpec((EXPERTS_PER_GROUP,) + wu.shape[1:], group),
            pl.BlockSpec((1,) + wd.shape[1:], group),
            pl.BlockSpec((1, d), const2),
        ],
        out_specs=pl.BlockSpec((tm, d), tile),
        out_shape=jax.ShapeDtypeStruct((n, d), F32),
        scratch_shapes=[
            pltpu.VMEM((tm, d), BF16),
            pltpu.VMEM((tm, ROUTER_LANES), F32),
            pltpu.VMEM((tm, d), F32),
            pltpu.VMEM((tm, ROUTER_LANES), F32),
            pltpu.VMEM((2, MOE_BLOCK, D_EXPERT), F32),
            pltpu.VMEM((2, MOE_BLOCK, D_EXPERT), F32),
            pltpu.VMEM((MOE_BLOCK, EXPERTS_PER_GROUP * D_EXPERT), BF16),
            pltpu.SMEM((8,), jnp.int32),
        ],
        compiler_params=pltpu.CompilerParams(
            dimension_semantics=("arbitrary", "arbitrary"), vmem_limit_bytes=VMEM_LIMIT),
        name="hier_moe",
    )(h, g, wr, br, triu, wg, wu, wd, gf)


def _router_params(wg_router, bg_router, we_router, be_router):
    d = wg_router.shape[0]
    we = jnp.transpose(we_router, (1, 0, 2)).reshape(d, N_EXPERTS)
    w = jnp.concatenate([wg_router, we], axis=1)
    w = jnp.pad(w, ((0, 0), (0, ROUTER_LANES - w.shape[1])))
    bias = jnp.concatenate([bg_router, be_router.reshape(N_EXPERTS)])
    bias = jnp.pad(bias, (0, ROUTER_LANES - bias.shape[0])).reshape(1, ROUTER_LANES)
    w_hi = w.astype(BF16)
    w_lo = (w - w_hi.astype(F32)).astype(BF16)
    return jnp.concatenate([w_hi, w_lo], axis=1), bias.astype(F32)


def _band_bias_t(rel_bias):
    width = PAIR_BAND + PAIR - 1
    m = jnp.arange(width + 1) - (PAIR - 1) - LEFT
    ext = rel_bias.astype(F32)[:, jnp.clip(m, -MAX_REL, MAX_REL) + MAX_REL]
    skew = jnp.tile(ext, (1, PAIR))[:, :PAIR * width].reshape(ATT_HEADS, PAIR, width)
    bias = skew[:, :, PAIR - 1:PAIR - 1 + PAIR_BAND]
    first = (jnp.arange(PAIR) // CHUNK * CHUNK)[:, None]
    kk = jnp.arange(PAIR_BAND)[None, :]
    bias = jnp.where(((kk >= first) & (kk < first + BAND))[None], bias * LOG2E, -1e30)
    bias_t = jnp.transpose(bias, (0, 2, 1)).reshape(ATT_HEADS // 2, 2, PAIR_BAND, PAIR)
    return jnp.transpose(bias_t, (0, 2, 1, 3)).reshape(ATT_HEADS // 2, PAIR_BAND, 2 * PAIR)


def kernel(x, norm_mix_g, norm_ffn_g, final_norm_g, ab_w_in, pool_w, pool_scale, att_rel_bias,
           ab_w_out, sgu_w_in, sgu_b_in, sgu_ln_g, sgu_ln_b, sgu_w_s, sgu_b_s, sgu_w_out,
           moe_wg_router, moe_bg_router, moe_we_router, moe_be_router,
           moe_w_gate, moe_w_up, moe_w_down):
    b, s, d = x.shape
    row = lambda a: a.reshape(1, -1).astype(F32)

    def moe(h, layer, final_norm):
        wr, br = _router_params(moe_wg_router[layer], moe_bg_router[layer],
                                moe_we_router[layer], moe_be_router[layer])
        wd = moe_w_down[layer].astype(BF16).reshape(N_GROUPS, EXPERTS_PER_GROUP * D_EXPERT, d)
        return _moe(h, row(norm_ffn_g[layer]), wr, br, moe_w_gate[layer].astype(BF16),
                    moe_w_up[layer].astype(BF16), wd, row(final_norm_g), final_norm)

    w_in = ab_w_in[0]
    n_pqk = POOL_WIDTH + 2 * ATT_WIDTH
    p, q, k, vt = _inproj(x, row(norm_mix_g[0]), w_in[:, :n_pqk].astype(BF16),
                          w_in[:, n_pqk:].T.astype(BF16))
    h = _mix(x, p, q, k, vt, _band_bias_t(att_rel_bias[0]), pool_w[0].astype(BF16),
             row(pool_scale[0]), ab_w_out[0].astype(BF16))
    h = moe(h.reshape(b * s, d), 0, False)

    h = _sgu(h, row(norm_mix_g[1]), sgu_w_in[0].astype(BF16), row(sgu_b_in[0]),
             row(sgu_ln_g[0]), row(sgu_ln_b[0]), sgu_w_s[0].astype(BF16),
             sgu_b_s[0].astype(F32)[:, :, None], sgu_w_out[0].astype(BF16))
    h = moe(h, 1, True)
    return h.reshape(b, s, d)
```

```python
import functools

import jax
import jax.numpy as jnp
from jax import lax
from jax.experimental import pallas as pl
from jax.experimental.pallas import tpu as pltpu

EPS = 1e-6
CHUNK = 64
POOL_WINDOWS = (2, 4, 8, 16)
POOL_GROUP_DIM = 128
POOL_WIDTH = 512
POOL_HALO = 16
ATT_HEADS = 8
ATT_HEAD_DIM = 64
ATT_WIDTH = 512
LEFT_CHUNKS = 8
LEFT = LEFT_CHUNKS * CHUNK
BAND = LEFT + CHUNK
PAIR = 2 * CHUNK
PAIR_BAND = LEFT + PAIR
MAX_REL = 128
ATT_SCALE = ATT_HEAD_DIM ** -0.5
LOG2E = 1.4426950408889634
VT_ONES = 16
VT_PAIR = 2 * ATT_HEAD_DIM + VT_ONES
VT_ROWS = (ATT_HEADS // 2) * VT_PAIR
SGU_CHUNK = 128
SGU_HEADS = 8
N_GROUPS = 4
EXPERTS_PER_GROUP = 4
N_EXPERTS = 16
D_EXPERT = 256
ROUTER_LANES = 128
MOE_TILE = 1024
MOE_BLOCK = 288
PERM_BLOCK = 256
ROW_ALIGN = 16

VMEM_LIMIT = 56 * 1024 * 1024

F32 = jnp.float32
BF16 = jnp.bfloat16


def _rms(x, g):
    return x * lax.rsqrt(jnp.mean(x * x, axis=-1, keepdims=True) + EPS) * g


def _inproj_kernel(x_ref, g_ref, w_ref, wvt_ref, p_ref, q_ref, k_ref, vt_ref):
    i = pl.program_id(1)

    @pl.when(i == 0)
    def _():
        k_ref[...] = jnp.zeros_like(k_ref)
        vt_ref[...] = jnp.zeros_like(vt_ref)

    @pl.when(i > 0)
    def _():
        xn = _rms(x_ref[0], g_ref[...]).astype(BF16)
        z = jnp.dot(xn, w_ref[...], preferred_element_type=F32)
        p_ref[0] = z[:, :POOL_WIDTH]
        q_ref[0] = (z[:, POOL_WIDTH:POOL_WIDTH + ATT_WIDTH] * (ATT_SCALE * LOG2E)).astype(BF16)
        k_ref[0] = z[:, POOL_WIDTH + ATT_WIDTH:].astype(BF16)
        vt = lax.dot_general(wvt_ref[...], xn, (((1,), (1,)), ((), ())), preferred_element_type=F32)
        for j in range(ATT_HEADS // 2):
            rows = slice(2 * j * ATT_HEAD_DIM, 2 * (j + 1) * ATT_HEAD_DIM)
            vt_ref[0, j * VT_PAIR:j * VT_PAIR + 2 * ATT_HEAD_DIM, :] = vt[rows].astype(BF16)
            vt_ref[0, j * VT_PAIR + 2 * ATT_HEAD_DIM:(j + 1) * VT_PAIR, :] = jnp.ones(
                (VT_ONES, vt.shape[1]), BF16)


def _inproj(x, g, w_pqk, w_vt):
    b, s, d = x.shape
    tm = LEFT
    nt = s // tm
    prev = lambda bi, i: (bi, jnp.maximum(i - 1, 0), 0)
    const2 = lambda bi, i: (0, 0)
    return pl.pallas_call(
        _inproj_kernel,
        grid=(b, nt + 1),
        in_specs=[
            pl.BlockSpec((1, tm, d), prev),
            pl.BlockSpec((1, d), const2),
            pl.BlockSpec(w_pqk.shape, const2),
            pl.BlockSpec(w_vt.shape, const2),
        ],
        out_specs=[
            pl.BlockSpec((1, tm, POOL_WIDTH), prev),
            pl.BlockSpec((1, tm, ATT_WIDTH), prev),
            pl.BlockSpec((1, tm, ATT_WIDTH), lambda bi, i: (bi, i, 0)),
            pl.BlockSpec((1, VT_ROWS, tm), lambda bi, i: (bi, 0, i)),
        ],
        out_shape=[
            jax.ShapeDtypeStruct((b, s, POOL_WIDTH), F32),
            jax.ShapeDtypeStruct((b, s, ATT_WIDTH), BF16),
            jax.ShapeDtypeStruct((b, s + LEFT, ATT_WIDTH), BF16),
            jax.ShapeDtypeStruct((b, VT_ROWS, s + LEFT), BF16),
        ],
        compiler_params=pltpu.CompilerParams(
            dimension_semantics=("arbitrary", "arbitrary"), vmem_limit_bytes=VMEM_LIMIT),
        name="inproj",
    )(x, g, w_pqk, w_vt)


def _attend_tile(q_ref, k_ref, vt_ref, biast_ref, mix, st_buf, i, tq, masked, rows_done):
    lane = lax.broadcasted_iota(jnp.int32, (PAIR, 2 * ATT_HEAD_DIM), 1)
    even = lane < ATT_HEAD_DIM
    zero = jnp.zeros((), BF16)
    units = [(cp, j) for cp in range(tq // PAIR) for j in range(ATT_HEADS // 2)]

    def band_base(cp):
        return pl.multiple_of((i * (tq // PAIR) + cp) * PAIR, PAIR)

    def scores(u):
        cp, j = units[u]
        cols = slice(2 * j * ATT_HEAD_DIM, 2 * (j + 1) * ATT_HEAD_DIM)
        q2 = q_ref[0, cp * PAIR:(cp + 1) * PAIR, cols]
        qs = jnp.concatenate([jnp.where(even, q2, zero), jnp.where(even, zero, q2)], axis=0)
        kb = k_ref[0, pl.ds(band_base(cp), PAIR_BAND), cols]
        st_buf[u % 2] = lax.dot_general(kb, qs, (((1,), (1,)), ((), ())),
                                        preferred_element_type=F32)

    scores(0)
    for u, (cp, j) in enumerate(units):
        if u + 1 < len(units):
            scores(u + 1)
        cols = slice(2 * j * ATT_HEAD_DIM, 2 * (j + 1) * ATT_HEAD_DIM)
        base = band_base(cp)
        st = st_buf[u % 2] + biast_ref[j]
        if masked:
            valid = (lax.broadcasted_iota(jnp.int32, (PAIR_BAND, 1), 0) + base - LEFT) >= 0
            st = jnp.where(valid, st, -1e30)
        m = jnp.max(st, axis=0, keepdims=True)
        e = jnp.exp2(st - m).astype(BF16)
        vtb = vt_ref[0, j * VT_PAIR:(j + 1) * VT_PAIR, pl.ds(base, PAIR_BAND)]
        ot = jnp.dot(vtb, e, preferred_element_type=F32)
        ot = ot[:2 * ATT_HEAD_DIM] * (1.0 / ot[2 * ATT_HEAD_DIM:2 * ATT_HEAD_DIM + 1])
        o2t = jnp.concatenate([ot[:ATT_HEAD_DIM, :PAIR], ot[ATT_HEAD_DIM:, PAIR:]], axis=0)
        mix[cp * PAIR:(cp + 1) * PAIR, POOL_WIDTH + 2 * j * ATT_HEAD_DIM:
            POOL_WIDTH + 2 * (j + 1) * ATT_HEAD_DIM] = o2t.T.astype(BF16)
        if j == ATT_HEADS // 2 - 1:
            rows_done(cp)


def _mix_kernel(x_ref, p_ref, halo_ref, q_ref, k_ref, vt_ref, biast_ref, wpool_ref, pscale_ref,
                wout_ref, o_ref, pbuf, mix, st_buf, *, tq):
    i = pl.program_id(1)

    def pool():
        halo = halo_ref[0]
        pbuf[0:POOL_HALO, :] = jnp.where(i == 0, jnp.zeros_like(halo), halo)
        pbuf[POOL_HALO:, :] = p_ref[0]
        t_abs = i * tq + lax.broadcasted_iota(jnp.int32, (tq, 1), 0)
        for gi, w in enumerate(POOL_WINDOWS):
            cols = slice(gi * POOL_GROUP_DIM, (gi + 1) * POOL_GROUP_DIM)
            cur = pbuf[POOL_HALO:, cols]
            acc = cur
            for j in range(1, w):
                acc = acc + pbuf[POOL_HALO - j:POOL_HALO - j + tq, cols]
            cnt = jnp.minimum(t_abs + 1, w).astype(F32)
            pooled = (acc / cnt - cur).astype(BF16)
            y = jnp.dot(pooled, wpool_ref[gi], preferred_element_type=F32) * pscale_ref[:, cols]
            mix[:, cols] = y.astype(BF16)

    def out_proj(cp):
        if cp % 2 == 1:
            rows = slice((cp - 1) * PAIR, (cp + 1) * PAIR)
            o_ref[0, rows, :] = x_ref[0, rows, :] + jnp.dot(mix[rows, :], wout_ref[...],
                                                          preferred_element_type=F32)

    def step(masked):
        pool()
        _attend_tile(q_ref, k_ref, vt_ref, biast_ref, mix, st_buf, i, tq, masked, out_proj)

    pl.when(i * tq < LEFT)(lambda: step(True))
    pl.when(i * tq >= LEFT)(lambda: step(False))


def _mix(x, p, q, k, vt, biast, w_pool, pool_scale, w_out):
    b, s, d = x.shape
    tq = 512
    halo_blocks = tq // POOL_HALO
    tile = lambda bi, i: (bi, i, 0)
    full = lambda bi, i: (bi, 0, 0)
    const2 = lambda bi, i: (0, 0)
    const3 = lambda bi, i: (0, 0, 0)
    return pl.pallas_call(
        functools.partial(_mix_kernel, tq=tq),
        grid=(b, s // tq),
        in_specs=[
            pl.BlockSpec((1, tq, d), tile),
            pl.BlockSpec((1, tq, POOL_WIDTH), tile),
            pl.BlockSpec((1, POOL_HALO, POOL_WIDTH),
                         lambda bi, i: (bi, jnp.maximum(i * halo_blocks - 1, 0), 0)),
            pl.BlockSpec((1, tq, ATT_WIDTH), tile),
            pl.BlockSpec((1, s + LEFT, ATT_WIDTH), full),
            pl.BlockSpec((1, VT_ROWS, s + LEFT), full),
            pl.BlockSpec(biast.shape, const3),
            pl.BlockSpec(w_pool.shape, const3),
            pl.BlockSpec(pool_scale.shape, const2),
            pl.BlockSpec(w_out.shape, const2),
        ],
        out_specs=pl.BlockSpec((1, tq, d), tile),
        out_shape=jax.ShapeDtypeStruct((b, s, d), F32),
        scratch_shapes=[
            pltpu.VMEM((POOL_HALO + tq, POOL_WIDTH), F32),
            pltpu.VMEM((tq, POOL_WIDTH + ATT_WIDTH), BF16),
            pltpu.VMEM((2, PAIR_BAND, 2 * PAIR), F32),
        ],
        compiler_params=pltpu.CompilerParams(
            dimension_semantics=("arbitrary", "arbitrary"), vmem_limit_bytes=VMEM_LIMIT),
        name="pool_attn_mix",
    )(x, p, p, q, k, vt, biast, w_pool, pool_scale, w_out)


def _sgu_kernel(h_ref, g_ref, win_ref, bin_ref, lng_ref, lnb_ref, ws_ref, bs_ref, wout_ref,
                o_ref, xn_ref, z_buf, gated, *, tm, sub):
    d = h_ref.shape[-1]
    hd = d // SGU_HEADS
    xn_ref[...] = _rms(h_ref[...], g_ref[...]).astype(BF16)
    row = lax.broadcasted_iota(jnp.int32, (SGU_CHUNK, SGU_CHUNK), 0)
    col = lax.broadcasted_iota(jnp.int32, (SGU_CHUNK, SGU_CHUNK), 1)
    ws = [jnp.where(row >= col, ws_ref[hh], jnp.zeros((), BF16)) for hh in range(SGU_HEADS)]

    def in_proj(s):
        z_buf[s % 2] = jnp.dot(xn_ref[s * sub:(s + 1) * sub, :], win_ref[...],
                               preferred_element_type=F32)

    in_proj(0)
    for s in range(tm // sub):
        if s + 1 < tm // sub:
            in_proj(s + 1)
        z = z_buf[s % 2] + bin_ref[...]
        z = 0.5 * z * (1.0 + lax.erf(z * (2.0 ** -0.5)))
        u = z[:, :d]
        v = z[:, d:]
        mu = jnp.mean(v, axis=-1, keepdims=True)
        vc = v - mu
        var = jnp.mean(vc * vc, axis=-1, keepdims=True)
        vn = (vc * lax.rsqrt(var + EPS) * lng_ref[...] + lnb_ref[...]).astype(BF16)
        for hh in range(SGU_HEADS):
            cols = slice(hh * hd, (hh + 1) * hd)
            for n in range(sub // SGU_CHUNK):
                rows = slice(n * SGU_CHUNK, (n + 1) * SGU_CHUNK)
                mixed = jnp.dot(ws[hh], vn[rows, cols], preferred_element_type=F32) + bs_ref[hh]
                gated[s * sub + n * SGU_CHUNK:s * sub + (n + 1) * SGU_CHUNK, cols] = (
                    u[rows, cols] * mixed).astype(BF16)
        blk = slice(s * sub, (s + 1) * sub)
        o_ref[blk, :] = h_ref[blk, :] + jnp.dot(gated[blk, :], wout_ref[...],
                                                preferred_element_type=F32)


def _sgu(h, g, w_in, b_in, ln_g, ln_b, w_s, b_s, w_out):
    n, d = h.shape
    tm, sub = 1024, 256
    tile = lambda i: (i, 0)
    const2 = lambda i: (0, 0)
    const3 = lambda i: (0, 0, 0)
    return pl.pallas_call(
        functools.partial(_sgu_kernel, tm=tm, sub=sub),
        grid=(n // tm,),
        in_specs=[
            pl.BlockSpec((tm, d), tile),
            pl.BlockSpec((1, d), const2),
            pl.BlockSpec(w_in.shape, const2),
            pl.BlockSpec(b_in.shape, const2),
            pl.BlockSpec(ln_g.shape, const2),
            pl.BlockSpec(ln_b.shape, const2),
            pl.BlockSpec(w_s.shape, const3),
            pl.BlockSpec(b_s.shape, const3),
            pl.BlockSpec(w_out.shape, const2),
        ],
        out_specs=pl.BlockSpec((tm, d), tile),
        out_shape=jax.ShapeDtypeStruct((n, d), F32),
        scratch_shapes=[
            pltpu.VMEM((tm, d), BF16),
            pltpu.VMEM((2, sub, 2 * d), F32),
            pltpu.VMEM((tm, d), BF16),
        ],
        compiler_params=pltpu.CompilerParams(
            dimension_semantics=("arbitrary",), vmem_limit_bytes=VMEM_LIMIT),
        name="sgu",
    )(h, g, w_in, b_in, ln_g, ln_b, w_s, b_s, w_out)


def _route_rows(lt):
    neg = jnp.float32(-jnp.inf)

    def first_max(rows):
        m = functools.reduce(jnp.maximum, rows)
        idx = jnp.int32(len(rows))
        for r in reversed(range(len(rows))):
            idx = jnp.where(rows[r] == m, jnp.int32(r), idx)
        return m, idx

    gl = [lt[g:g + 1, :] for g in range(N_GROUPS)]
    gmax, gidx = first_max(gl)
    g_w = 1.0 / functools.reduce(jnp.add, [jnp.exp(x - gmax) for x in gl])
    el = []
    for j in range(EXPERTS_PER_GROUP):
        row = lt[N_GROUPS + j:N_GROUPS + j + 1, :]
        for g in range(1, N_GROUPS):
            r = N_GROUPS + EXPERTS_PER_GROUP * g + j
            row = jnp.where(gidx == g, lt[r:r + 1, :], row)
        el.append(row)
    m1, i1 = first_max(el)
    m2, i2 = first_max([jnp.where(i1 == j, neg, el[j]) for j in range(EXPERTS_PER_GROUP)])
    e2 = jnp.exp(m2 - m1)
    den = 1.0 + e2
    w1 = g_w / den
    w2 = g_w * e2 / den
    cw = [jnp.where(i1 == j, w1, 0.0) + jnp.where(i2 == j, w2, 0.0) for j in range(EXPERTS_PER_GROUP)]
    return gidx, cw


def _moe_kernel(h_ref, g_ref, wr_ref, br_ref, triu_ref, wg_ref, wu_ref, wd_ref, gf_ref,
                o_ref, xs_ref, ws_ref, ys_ref, pos_ref, hg_buf, hu_buf, act_ref, bounds, *, final_norm):
    grp = pl.program_id(1)
    tm, d = h_ref.shape
    nperm = tm // PERM_BLOCK

    @pl.when(grp == 0)
    def _():
        xn = _rms(h_ref[...], g_ref[...])
        x_hi = xn.astype(BF16)
        x_lo = (xn - x_hi.astype(F32)).astype(BF16)
        both = jnp.dot(x_hi, wr_ref[...], preferred_element_type=F32)
        logits = (both[:, :ROUTER_LANES] + both[:, ROUTER_LANES:]
                  + jnp.dot(x_lo, wr_ref[:, :ROUTER_LANES], preferred_element_type=F32)) + br_ref[...]
        gidx, cw = _route_rows(logits.T)

        row16 = lax.broadcasted_iota(jnp.int32, (ROW_ALIGN, tm), 0)
        onehot = jnp.where(row16 == gidx, 1.0, 0.0)
        rank = jnp.dot(onehot.astype(BF16), triu_ref[...], preferred_element_type=F32)
        pos_row = jnp.zeros((1, tm), F32)
        start = jnp.zeros((1, 1), F32)
        for g in range(N_GROUPS):
            bounds[g] = start[0, 0].astype(jnp.int32)
            mine = gidx == g
            pos_row = jnp.where(mine, rank[g:g + 1, :] + start, pos_row)
            start = start + jnp.sum(jnp.where(mine, 1.0, 0.0), axis=-1, keepdims=True)
        bounds[N_GROUPS] = jnp.int32(tm)

        packed = jnp.where(row16 == 0, pos_row, 0.0)
        for j in range(EXPERTS_PER_GROUP):
            piece = cw[j]
            for p in range(3):
                part = piece.astype(BF16).astype(F32)
                packed = packed + jnp.where(row16 == 1 + p * EXPERTS_PER_GROUP + j, part, 0.0)
                piece = piece - part
        packed = jnp.concatenate(
            [packed, jnp.zeros((ROUTER_LANES - ROW_ALIGN, tm), F32)], axis=0)
        tok = packed.T
        pos_ref[...] = tok
        w3 = tok.astype(BF16)
        pos_row = pos_row.astype(jnp.int32)

        for rb in range(nperm):
            rows = slice(rb * PERM_BLOCK, (rb + 1) * PERM_BLOCK)
            sorted_row = lax.broadcasted_iota(jnp.int32, (PERM_BLOCK, tm), 0) + rb * PERM_BLOCK
            perm = jnp.where(sorted_row == pos_row, 1.0, 0.0).astype(BF16)
            xs_ref[rows, :] = jnp.dot(perm, x_hi, preferred_element_type=F32).astype(BF16)
            ws_ref[rows, :] = jnp.dot(perm, w3, preferred_element_type=F32)
        ys_ref[...] = jnp.zeros_like(ys_ref)

    seg_lo = bounds[grp]
    seg_hi = bounds[grp + 1]
    first = (seg_lo // ROW_ALIGN) * ROW_ALIGN
    lane_b = lax.broadcasted_iota(jnp.int32, (MOE_BLOCK, ROUTER_LANES), 1)
    for k in range(pl.cdiv(tm, MOE_BLOCK)):
        blk_lo = first + k * MOE_BLOCK

        @pl.when(blk_lo < seg_hi)
        def _(blk_lo=blk_lo):
            row0 = pl.multiple_of(jnp.minimum(blk_lo, tm - MOE_BLOCK), ROW_ALIGN)
            rows = pl.ds(row0, MOE_BLOCK)
            sorted_row = lax.broadcasted_iota(jnp.int32, (MOE_BLOCK, 1), 0) + row0
            in_seg = ((sorted_row >= jnp.maximum(seg_lo, blk_lo))
                      & (sorted_row < jnp.minimum(seg_hi, blk_lo + MOE_BLOCK)))
            w3 = ws_ref[rows, :]

            def gate_up(j):
                hg_buf[j % 2] = jnp.dot(xs_ref[rows, :], wg_ref[j], preferred_element_type=F32)
                hu_buf[j % 2] = jnp.dot(xs_ref[rows, :], wu_ref[j], preferred_element_type=F32)

            gate_up(0)
            for j in range(EXPERTS_PER_GROUP):
                if j + 1 < EXPERTS_PER_GROUP:
                    gate_up(j + 1)
                mine = (((lane_b - 1) % EXPERTS_PER_GROUP == j) & (lane_b >= 1)
                        & (lane_b <= 3 * EXPERTS_PER_GROUP))
                c = jnp.sum(jnp.where(mine, w3, 0.0), axis=-1, keepdims=True)
                c = jnp.where(in_seg, c, 0.0)
                hg = hg_buf[j % 2]
                act = hg * (1.0 / (1.0 + jnp.exp(-hg))) * hu_buf[j % 2] * c
                act_ref[:, j * D_EXPERT:(j + 1) * D_EXPERT] = act.astype(BF16)
            ys_ref[rows, :] += jnp.dot(act_ref[...], wd_ref[0], preferred_element_type=F32)

    @pl.when(grp == N_GROUPS - 1)
    def _():
        xs_ref[...] = ys_ref[...].astype(BF16)
        for tb in range(nperm):
            rows = slice(tb * PERM_BLOCK, (tb + 1) * PERM_BLOCK)
            pos_i = pos_ref[rows, :1].astype(jnp.int32)
            perm_t = jnp.where(lax.broadcasted_iota(jnp.int32, (PERM_BLOCK, tm), 1) == pos_i,
                               1.0, 0.0).astype(BF16)
            out = h_ref[rows, :] + jnp.dot(perm_t, xs_ref[...], preferred_element_type=F32)
            if final_norm:
                out = _rms(out, gf_ref[...])
            o_ref[rows, :] = out


def _moe(h, g, wr, br, wg, wu, wd, gf, final_norm):
    n, d = h.shape
    tm = MOE_TILE
    tile = lambda i, e: (i, 0)
    const2 = lambda i, e: (0, 0)
    group = lambda i, e: (e, 0, 0)
    triu = jnp.tri(tm, tm, -1, dtype=BF16).T
    return pl.pallas_call(
        functools.partial(_moe_kernel, final_norm=final_norm),
        grid=(n // tm, N_GROUPS),
        in_specs=[
            pl.BlockSpec((tm, d), tile),
            pl.BlockSpec((1, d), const2),
            pl.BlockSpec(wr.shape, const2),
            pl.BlockSpec(br.shape, const2),
            pl.BlockSpec(triu.shape, const2),
            pl.BlockSpec((EXPERTS_PER_GROUP,) + wg.shape[1:], group),
            pl.BlockSpec((EXPERTS_PER_GROUP,) + wu.shape[1:], group),
            pl.BlockSpec((1,) + wd.shape[1:], group),
            pl.BlockSpec((1, d), const2),
        ],
        out_specs=pl.BlockSpec((tm, d), tile),
        out_shape=jax.ShapeDtypeStruct((n, d), F32),
        scratch_shapes=[
            pltpu.VMEM((tm, d), BF16),
            pltpu.VMEM((tm, ROUTER_LANES), F32),
            pltpu.VMEM((tm, d), F32),
            pltpu.VMEM((tm, ROUTER_LANES), F32),
            pltpu.VMEM((2, MOE_BLOCK, D_EXPERT), F32),
            pltpu.VMEM((2, MOE_BLOCK, D_EXPERT), F32),
            pltpu.VMEM((MOE_BLOCK, EXPERTS_PER_GROUP * D_EXPERT), BF16),
            pltpu.SMEM((8,), jnp.int32),
        ],
        compiler_params=pltpu.CompilerParams(
            dimension_semantics=("arbitrary", "arbitrary"), vmem_limit_bytes=VMEM_LIMIT),
        name="hier_moe",
    )(h, g, wr, br, triu, wg, wu, wd, gf)


def _router_params(wg_router, bg_router, we_router, be_router):
    d = wg_router.shape[0]
    we = jnp.transpose(we_router, (1, 0, 2)).reshape(d, N_EXPERTS)
    w = jnp.concatenate([wg_router, we], axis=1)
    w = jnp.pad(w, ((0, 0), (0, ROUTER_LANES - w.shape[1])))
    bias = jnp.concatenate([bg_router, be_router.reshape(N_EXPERTS)])
    bias = jnp.pad(bias, (0, ROUTER_LANES - bias.shape[0])).reshape(1, ROUTER_LANES)
    w_hi = w.astype(BF16)
    w_lo = (w - w_hi.astype(F32)).astype(BF16)
    return jnp.concatenate([w_hi, w_lo], axis=1), bias.astype(F32)


def _band_bias_t(rel_bias):
    width = PAIR_BAND + PAIR - 1
    m = jnp.arange(width + 1) - (PAIR - 1) - LEFT
    ext = rel_bias.astype(F32)[:, jnp.clip(m, -MAX_REL, MAX_REL) + MAX_REL]
    skew = jnp.tile(ext, (1, PAIR))[:, :PAIR * width].reshape(ATT_HEADS, PAIR, width)
    bias = skew[:, :, PAIR - 1:PAIR - 1 + PAIR_BAND]
    first = (jnp.arange(PAIR) // CHUNK * CHUNK)[:, None]
    kk = jnp.arange(PAIR_BAND)[None, :]
    bias = jnp.where(((kk >= first) & (kk < first + BAND))[None], bias * LOG2E, -1e30)
    bias_t = jnp.transpose(bias, (0, 2, 1)).reshape(ATT_HEADS // 2, 2, PAIR_BAND, PAIR)
    return jnp.transpose(bias_t, (0, 2, 1, 3)).reshape(ATT_HEADS // 2, PAIR_BAND, 2 * PAIR)


def kernel(x, norm_mix_g, norm_ffn_g, final_norm_g, ab_w_in, pool_w, pool_scale, att_rel_bias,
           ab_w_out, sgu_w_in, sgu_b_in, sgu_ln_g, sgu_ln_b, sgu_w_s, sgu_b_s, sgu_w_out,
           moe_wg_router, moe_bg_router, moe_we_router, moe_be_router,
           moe_w_gate, moe_w_up, moe_w_down):
    b, s, d = x.shape
    row = lambda a: a.reshape(1, -1).astype(F32)

    def moe(h, layer, final_norm):
        wr, br = _router_params(moe_wg_router[layer], moe_bg_router[layer],
                                moe_we_router[layer], moe_be_router[layer])
        wd = moe_w_down[layer].astype(BF16).reshape(N_GROUPS, EXPERTS_PER_GROUP * D_EXPERT, d)
        return _moe(h, row(norm_ffn_g[layer]), wr, br, moe_w_gate[layer].astype(BF16),
                    moe_w_up[layer].astype(BF16), wd, row(final_norm_g), final_norm)

    w_in = ab_w_in[0]
    n_pqk = POOL_WIDTH + 2 * ATT_WIDTH
    p, q, k, vt = _inproj(x, row(norm_mix_g[0]), w_in[:, :n_pqk].astype(BF16),
                          w_in[:, n_pqk:].T.astype(BF16))
    h = _mix(x, p, q, k, vt, _band_bias_t(att_rel_bias[0]), pool_w[0].astype(BF16),
             row(pool_scale[0]), ab_w_out[0].astype(BF16))
    h = moe(h.reshape(b * s, d), 0, False)

    h = _sgu(h, row(norm_mix_g[1]), sgu_w_in[0].astype(BF16), row(sgu_b_in[0]),
             row(sgu_ln_g[0]), row(sgu_ln_b[0]), sgu_w_s[0].astype(BF16),
             sgu_b_s[0].astype(F32)[:, :, None], sgu_w_out[0].astype(BF16))
    h = moe(h, 1, True)
    return h.reshape(b, s, d)
```

```python
import functools

import jax
import jax.numpy as jnp
from jax import lax
from jax.experimental import pallas as pl
from jax.experimental.pallas import tpu as pltpu

EPS = 1e-6
CHUNK = 64
POOL_WINDOWS = (2, 4, 8, 16)
POOL_GROUP_DIM = 128
POOL_WIDTH = 512
POOL_HALO = 16
ATT_HEADS = 8
ATT_HEAD_DIM = 64
ATT_WIDTH = 512
LEFT_CHUNKS = 8
LEFT = LEFT_CHUNKS * CHUNK
BAND = LEFT + CHUNK
PAIR = 2 * CHUNK
PAIR_BAND = LEFT + PAIR
MAX_REL = 128
ATT_SCALE = ATT_HEAD_DIM ** -0.5
LOG2E = 1.4426950408889634
VT_ONES = 16
VT_PAIR = 2 * ATT_HEAD_DIM + VT_ONES
VT_ROWS = (ATT_HEADS // 2) * VT_PAIR
SGU_CHUNK = 128
SGU_HEADS = 8
N_GROUPS = 4
EXPERTS_PER_GROUP = 4
N_EXPERTS = 16
D_EXPERT = 256
ROUTER_LANES = 128
MOE_TILE = 1024
MOE_BLOCK = 320
PERM_BLOCK = 256
ROW_ALIGN = 16

VMEM_LIMIT = 56 * 1024 * 1024

F32 = jnp.float32
BF16 = jnp.bfloat16


def _rms(x, g):
    return x * lax.rsqrt(jnp.mean(x * x, axis=-1, keepdims=True) + EPS) * g


def _inproj_kernel(x_ref, g_ref, w_ref, wvt_ref, p_ref, q_ref, k_ref, vt_ref):
    i = pl.program_id(1)

    @pl.when(i == 0)
    def _():
        k_ref[...] = jnp.zeros_like(k_ref)
        vt_ref[...] = jnp.zeros_like(vt_ref)

    @pl.when(i > 0)
    def _():
        xn = _rms(x_ref[0], g_ref[...]).astype(BF16)
        z = jnp.dot(xn, w_ref[...], preferred_element_type=F32)
        p_ref[0] = z[:, :POOL_WIDTH]
        q_ref[0] = (z[:, POOL_WIDTH:POOL_WIDTH + ATT_WIDTH] * (ATT_SCALE * LOG2E)).astype(BF16)
        k_ref[0] = z[:, POOL_WIDTH + ATT_WIDTH:].astype(BF16)
        vt = lax.dot_general(wvt_ref[...], xn, (((1,), (1,)), ((), ())), preferred_element_type=F32)
        for j in range(ATT_HEADS // 2):
            rows = slice(2 * j * ATT_HEAD_DIM, 2 * (j + 1) * ATT_HEAD_DIM)
            vt_ref[0, j * VT_PAIR:j * VT_PAIR + 2 * ATT_HEAD_DIM, :] = vt[rows].astype(BF16)
            vt_ref[0, j * VT_PAIR + 2 * ATT_HEAD_DIM:(j + 1) * VT_PAIR, :] = jnp.ones(
                (VT_ONES, vt.shape[1]), BF16)


def _inproj(x, g, w_pqk, w_vt):
    b, s, d = x.shape
    tm = LEFT
    nt = s // tm
    prev = lambda bi, i: (bi, jnp.maximum(i - 1, 0), 0)
    const2 = lambda bi, i: (0, 0)
    return pl.pallas_call(
        _inproj_kernel,
        grid=(b, nt + 1),
        in_specs=[
            pl.BlockSpec((1, tm, d), prev),
            pl.BlockSpec((1, d), const2),
            pl.BlockSpec(w_pqk.shape, const2),
            pl.BlockSpec(w_vt.shape, const2),
        ],
        out_specs=[
            pl.BlockSpec((1, tm, POOL_WIDTH), prev),
            pl.BlockSpec((1, tm, ATT_WIDTH), prev),
            pl.BlockSpec((1, tm, ATT_WIDTH), lambda bi, i: (bi, i, 0)),
            pl.BlockSpec((1, VT_ROWS, tm), lambda bi, i: (bi, 0, i)),
        ],
        out_shape=[
            jax.ShapeDtypeStruct((b, s, POOL_WIDTH), F32),
            jax.ShapeDtypeStruct((b, s, ATT_WIDTH), BF16),
            jax.ShapeDtypeStruct((b, s + LEFT, ATT_WIDTH), BF16),
            jax.ShapeDtypeStruct((b, VT_ROWS, s + LEFT), BF16),
        ],
        compiler_params=pltpu.CompilerParams(
            dimension_semantics=("arbitrary", "arbitrary"), vmem_limit_bytes=VMEM_LIMIT),
        name="inproj",
    )(x, g, w_pqk, w_vt)


def _attend_tile(q_ref, k_ref, vt_ref, biast_ref, mix, st_buf, i, tq, masked, rows_done):
    lane = lax.broadcasted_iota(jnp.int32, (PAIR, 2 * ATT_HEAD_DIM), 1)
    even = lane < ATT_HEAD_DIM
    zero = jnp.zeros((), BF16)
    units = [(cp, j) for cp in range(tq // PAIR) for j in range(ATT_HEADS // 2)]

    def band_base(cp):
        return pl.multiple_of((i * (tq // PAIR) + cp) * PAIR, PAIR)

    def scores(u):
        cp, j = units[u]
        cols = slice(2 * j * ATT_HEAD_DIM, 2 * (j + 1) * ATT_HEAD_DIM)
        q2 = q_ref[0, cp * PAIR:(cp + 1) * PAIR, cols]
        qs = jnp.concatenate([jnp.where(even, q2, zero), jnp.where(even, zero, q2)], axis=0)
        kb = k_ref[0, pl.ds(band_base(cp), PAIR_BAND), cols]
        st_buf[u % 2] = lax.dot_general(kb, qs, (((1,), (1,)), ((), ())),
                                        preferred_element_type=F32)

    scores(0)
    for u, (cp, j) in enumerate(units):
        if u + 1 < len(units):
            scores(u + 1)
        cols = slice(2 * j * ATT_HEAD_DIM, 2 * (j + 1) * ATT_HEAD_DIM)
        base = band_base(cp)
        st = st_buf[u % 2] + biast_ref[j]
        if masked:
            valid = (lax.broadcasted_iota(jnp.int32, (PAIR_BAND, 1), 0) + base - LEFT) >= 0
            st = jnp.where(valid, st, -1e30)
        m = jnp.max(st, axis=0, keepdims=True)
        e = jnp.exp2(st - m).astype(BF16)
        vtb = vt_ref[0, j * VT_PAIR:(j + 1) * VT_PAIR, pl.ds(base, PAIR_BAND)]
        ot = jnp.dot(vtb, e, preferred_element_type=F32)
        ot = ot[:2 * ATT_HEAD_DIM] * (1.0 / ot[2 * ATT_HEAD_DIM:2 * ATT_HEAD_DIM + 1])
        o2t = jnp.concatenate([ot[:ATT_HEAD_DIM, :PAIR], ot[ATT_HEAD_DIM:, PAIR:]], axis=0)
        mix[cp * PAIR:(cp + 1) * PAIR, POOL_WIDTH + 2 * j * ATT_HEAD_DIM:
            POOL_WIDTH + 2 * (j + 1) * ATT_HEAD_DIM] = o2t.T.astype(BF16)
        if j == ATT_HEADS // 2 - 1:
            rows_done(cp)


def _mix_kernel(x_ref, p_ref, halo_ref, q_ref, k_ref, vt_ref, biast_ref, wpool_ref, pscale_ref,
                wout_ref, o_ref, pbuf, mix, st_buf, *, tq):
    i = pl.program_id(1)

    def pool():
        halo = halo_ref[0]
        pbuf[0:POOL_HALO, :] = jnp.where(i == 0, jnp.zeros_like(halo), halo)
        pbuf[POOL_HALO:, :] = p_ref[0]
        t_abs = i * tq + lax.broadcasted_iota(jnp.int32, (tq, 1), 0)
        for gi, w in enumerate(POOL_WINDOWS):
            cols = slice(gi * POOL_GROUP_DIM, (gi + 1) * POOL_GROUP_DIM)
            cur = pbuf[POOL_HALO:, cols]
            acc = pbuf[:, cols]
            step = 1
            while step < w:
                acc = acc + pltpu.roll(acc, step, 0)
                step *= 2
            cnt = jnp.minimum(t_abs + 1, w).astype(F32)
            pooled = (acc[POOL_HALO:] / cnt - cur).astype(BF16)
            y = jnp.dot(pooled, wpool_ref[gi], preferred_element_type=F32) * pscale_ref[:, cols]
            mix[:, cols] = y.astype(BF16)

    def out_proj(cp):
        if cp % 2 == 1:
            rows = slice((cp - 1) * PAIR, (cp + 1) * PAIR)
            o_ref[0, rows, :] = x_ref[0, rows, :] + jnp.dot(mix[rows, :], wout_ref[...],
                                                          preferred_element_type=F32)

    def step(masked):
        pool()
        _attend_tile(q_ref, k_ref, vt_ref, biast_ref, mix, st_buf, i, tq, masked, out_proj)

    pl.when(i * tq < LEFT)(lambda: step(True))
    pl.when(i * tq >= LEFT)(lambda: step(False))


def _mix(x, p, q, k, vt, biast, w_pool, pool_scale, w_out):
    b, s, d = x.shape
    tq = 512
    halo_blocks = tq // POOL_HALO
    tile = lambda bi, i: (bi, i, 0)
    full = lambda bi, i: (bi, 0, 0)
    const2 = lambda bi, i: (0, 0)
    const3 = lambda bi, i: (0, 0, 0)
    return pl.pallas_call(
        functools.partial(_mix_kernel, tq=tq),
        grid=(b, s // tq),
        in_specs=[
            pl.BlockSpec((1, tq, d), tile),
            pl.BlockSpec((1, tq, POOL_WIDTH), tile),
            pl.BlockSpec((1, POOL_HALO, POOL_WIDTH),
                         lambda bi, i: (bi, jnp.maximum(i * halo_blocks - 1, 0), 0)),
            pl.BlockSpec((1, tq, ATT_WIDTH), tile),
            pl.BlockSpec((1, s + LEFT, ATT_WIDTH), full),
            pl.BlockSpec((1, VT_ROWS, s + LEFT), full),
            pl.BlockSpec(biast.shape, const3),
            pl.BlockSpec(w_pool.shape, const3),
            pl.BlockSpec(pool_scale.shape, const2),
            pl.BlockSpec(w_out.shape, const2),
        ],
        out_specs=pl.BlockSpec((1, tq, d), tile),
        out_shape=jax.ShapeDtypeStruct((b, s, d), F32),
        scratch_shapes=[
            pltpu.VMEM((POOL_HALO + tq, POOL_WIDTH), F32),
            pltpu.VMEM((tq, POOL_WIDTH + ATT_WIDTH), BF16),
            pltpu.VMEM((2, PAIR_BAND, 2 * PAIR), F32),
        ],
        compiler_params=pltpu.CompilerParams(
            dimension_semantics=("arbitrary", "arbitrary"), vmem_limit_bytes=VMEM_LIMIT),
        name="pool_attn_mix",
    )(x, p, p, q, k, vt, biast, w_pool, pool_scale, w_out)


def _sgu_kernel(h_ref, g_ref, win_ref, bin_ref, lng_ref, lnb_ref, ws_ref, bs_ref, wout_ref,
                o_ref, xn_ref, z_buf, gated, *, tm, sub):
    d = h_ref.shape[-1]
    hd = d // SGU_HEADS
    xn_ref[...] = _rms(h_ref[...], g_ref[...]).astype(BF16)
    row = lax.broadcasted_iota(jnp.int32, (SGU_CHUNK, SGU_CHUNK), 0)
    col = lax.broadcasted_iota(jnp.int32, (SGU_CHUNK, SGU_CHUNK), 1)
    ws = [jnp.where(row >= col, ws_ref[hh], jnp.zeros((), BF16)) for hh in range(SGU_HEADS)]

    def in_proj(s):
        z_buf[s % 2] = jnp.dot(xn_ref[s * sub:(s + 1) * sub, :], win_ref[...],
                               preferred_element_type=F32)

    in_proj(0)
    for s in range(tm // sub):
        if s + 1 < tm // sub:
            in_proj(s + 1)
        z = z_buf[s % 2] + bin_ref[...]
        z = 0.5 * z * (1.0 + lax.erf(z * (2.0 ** -0.5)))
        u = z[:, :d]
        v = z[:, d:]
        mu = jnp.mean(v, axis=-1, keepdims=True)
        vc = v - mu
        var = jnp.mean(vc * vc, axis=-1, keepdims=True)
        vn = (vc * lax.rsqrt(var + EPS) * lng_ref[...] + lnb_ref[...]).astype(BF16)
        for hh in range(SGU_HEADS):
            cols = slice(hh * hd, (hh + 1) * hd)
            for n in range(sub // SGU_CHUNK):
                rows = slice(n * SGU_CHUNK, (n + 1) * SGU_CHUNK)
                mixed = jnp.dot(ws[hh], vn[rows, cols], preferred_element_type=F32) + bs_ref[hh]
                gated[s * sub + n * SGU_CHUNK:s * sub + (n + 1) * SGU_CHUNK, cols] = (
                    u[rows, cols] * mixed).astype(BF16)
        blk = slice(s * sub, (s + 1) * sub)
        o_ref[blk, :] = h_ref[blk, :] + jnp.dot(gated[blk, :], wout_ref[...],
                                                preferred_element_type=F32)


def _sgu(h, g, w_in, b_in, ln_g, ln_b, w_s, b_s, w_out):
    n, d = h.shape
    tm, sub = 1024, 256
    tile = lambda i: (i, 0)
    const2 = lambda i: (0, 0)
    const3 = lambda i: (0, 0, 0)
    return pl.pallas_call(
        functools.partial(_sgu_kernel, tm=tm, sub=sub),
        grid=(n // tm,),
        in_specs=[
            pl.BlockSpec((tm, d), tile),
            pl.BlockSpec((1, d), const2),
            pl.BlockSpec(w_in.shape, const2),
            pl.BlockSpec(b_in.shape, const2),
            pl.BlockSpec(ln_g.shape, const2),
            pl.BlockSpec(ln_b.shape, const2),
            pl.BlockSpec(w_s.shape, const3),
            pl.BlockSpec(b_s.shape, const3),
            pl.BlockSpec(w_out.shape, const2),
        ],
        out_specs=pl.BlockSpec((tm, d), tile),
        out_shape=jax.ShapeDtypeStruct((n, d), F32),
        scratch_shapes=[
            pltpu.VMEM((tm, d), BF16),
            pltpu.VMEM((2, sub, 2 * d), F32),
            pltpu.VMEM((tm, d), BF16),
        ],
        compiler_params=pltpu.CompilerParams(
            dimension_semantics=("arbitrary",), vmem_limit_bytes=VMEM_LIMIT),
        name="sgu",
    )(h, g, w_in, b_in, ln_g, ln_b, w_s, b_s, w_out)


def _route_rows(lt):
    neg = jnp.float32(-jnp.inf)

    def first_max(rows):
        m = functools.reduce(jnp.maximum, rows)
        idx = jnp.int32(len(rows))
        for r in reversed(range(len(rows))):
            idx = jnp.where(rows[r] == m, jnp.int32(r), idx)
        return m, idx

    gl = [lt[g:g + 1, :] for g in range(N_GROUPS)]
    gmax, gidx = first_max(gl)
    g_w = 1.0 / functools.reduce(jnp.add, [jnp.exp(x - gmax) for x in gl])
    el = []
    for j in range(EXPERTS_PER_GROUP):
        row = lt[N_GROUPS + j:N_GROUPS + j + 1, :]
        for g in range(1, N_GROUPS):
            r = N_GROUPS + EXPERTS_PER_GROUP * g + j
            row = jnp.where(gidx == g, lt[r:r + 1, :], row)
        el.append(row)
    m1, i1 = first_max(el)
    m2, i2 = first_max([jnp.where(i1 == j, neg, el[j]) for j in range(EXPERTS_PER_GROUP)])
    e2 = jnp.exp(m2 - m1)
    den = 1.0 + e2
    w1 = g_w / den
    w2 = g_w * e2 / den
    cw = [jnp.where(i1 == j, w1, 0.0) + jnp.where(i2 == j, w2, 0.0) for j in range(EXPERTS_PER_GROUP)]
    return gidx, cw


def _moe_kernel(h_ref, g_ref, wr_ref, br_ref, triu_ref, wg_ref, wu_ref, wd_ref, gf_ref,
                o_ref, xs_ref, ws_ref, ys_ref, pos_ref, hg_buf, hu_buf, act_ref, bounds, *, final_norm):
    grp = pl.program_id(1)
    tm, d = h_ref.shape
    nperm = tm // PERM_BLOCK

    @pl.when(grp == 0)
    def _():
        xn = _rms(h_ref[...], g_ref[...])
        x_hi = xn.astype(BF16)
        x_lo = (xn - x_hi.astype(F32)).astype(BF16)
        both = jnp.dot(x_hi, wr_ref[...], preferred_element_type=F32)
        logits = (both[:, :ROUTER_LANES] + both[:, ROUTER_LANES:]
                  + jnp.dot(x_lo, wr_ref[:, :ROUTER_LANES], preferred_element_type=F32)) + br_ref[...]
        gidx, cw = _route_rows(logits.T)

        row16 = lax.broadcasted_iota(jnp.int32, (ROW_ALIGN, tm), 0)
        onehot = jnp.where(row16 == gidx, 1.0, 0.0)
        rank = jnp.dot(onehot.astype(BF16), triu_ref[...], preferred_element_type=F32)
        pos_row = jnp.zeros((1, tm), F32)
        start = jnp.zeros((1, 1), F32)
        for g in range(N_GROUPS):
            bounds[g] = start[0, 0].astype(jnp.int32)
            mine = gidx == g
            pos_row = jnp.where(mine, rank[g:g + 1, :] + start, pos_row)
            start = start + jnp.sum(jnp.where(mine, 1.0, 0.0), axis=-1, keepdims=True)
        bounds[N_GROUPS] = jnp.int32(tm)

        packed = jnp.where(row16 == 0, pos_row, 0.0)
        for j in range(EXPERTS_PER_GROUP):
            piece = cw[j]
            for p in range(3):
                part = piece.astype(BF16).astype(F32)
                packed = packed + jnp.where(row16 == 1 + p * EXPERTS_PER_GROUP + j, part, 0.0)
                piece = piece - part
        packed = jnp.concatenate(
            [packed, jnp.zeros((ROUTER_LANES - ROW_ALIGN, tm), F32)], axis=0)
        tok = packed.T
        pos_ref[...] = tok
        w3 = tok.astype(BF16)
        pos_row = pos_row.astype(jnp.int32)

        for rb in range(nperm):
            rows = slice(rb * PERM_BLOCK, (rb + 1) * PERM_BLOCK)
            sorted_row = lax.broadcasted_iota(jnp.int32, (PERM_BLOCK, tm), 0) + rb * PERM_BLOCK
            perm = jnp.where(sorted_row == pos_row, 1.0, 0.0).astype(BF16)
            xs_ref[rows, :] = jnp.dot(perm, x_hi, preferred_element_type=F32).astype(BF16)
            ws_ref[rows, :] = jnp.dot(perm, w3, preferred_element_type=F32)
        ys_ref[...] = jnp.zeros_like(ys_ref)

    seg_lo = bounds[grp]
    seg_hi = bounds[grp + 1]
    first = (seg_lo // ROW_ALIGN) * ROW_ALIGN
    lane_b = lax.broadcasted_iota(jnp.int32, (MOE_BLOCK, ROUTER_LANES), 1)
    for k in range(pl.cdiv(tm, MOE_BLOCK)):
        blk_lo = first + k * MOE_BLOCK

        @pl.when(blk_lo < seg_hi)
        def _(blk_lo=blk_lo):
            row0 = pl.multiple_of(jnp.minimum(blk_lo, tm - MOE_BLOCK), ROW_ALIGN)
            rows = pl.ds(row0, MOE_BLOCK)
            sorted_row = lax.broadcasted_iota(jnp.int32, (MOE_BLOCK, 1), 0) + row0
            in_seg = ((sorted_row >= jnp.maximum(seg_lo, blk_lo))
                      & (sorted_row < jnp.minimum(seg_hi, blk_lo + MOE_BLOCK)))
            w3 = ws_ref[rows, :]

            def gate_up(j):
                hg_buf[j % 2] = jnp.dot(xs_ref[rows, :], wg_ref[j], preferred_element_type=F32)
                hu_buf[j % 2] = jnp.dot(xs_ref[rows, :], wu_ref[j], preferred_element_type=F32)

            gate_up(0)
            for j in range(EXPERTS_PER_GROUP):
                if j + 1 < EXPERTS_PER_GROUP:
                    gate_up(j + 1)
                mine = (((lane_b - 1) % EXPERTS_PER_GROUP == j) & (lane_b >= 1)
                        & (lane_b <= 3 * EXPERTS_PER_GROUP))
                c = jnp.sum(jnp.where(mine, w3, 0.0), axis=-1, keepdims=True)
                c = jnp.where(in_seg, c, 0.0)
                hg = hg_buf[j % 2]
                act = hg * (1.0 / (1.0 + jnp.exp(-hg))) * hu_buf[j % 2] * c
                act_ref[:, j * D_EXPERT:(j + 1) * D_EXPERT] = act.astype(BF16)
            ys_ref[rows, :] += jnp.dot(act_ref[...], wd_ref[0], preferred_element_type=F32)

    @pl.when(grp == N_GROUPS - 1)
    def _():
        xs_ref[...] = ys_ref[...].astype(BF16)
        for tb in range(nperm):
            rows = slice(tb * PERM_BLOCK, (tb + 1) * PERM_BLOCK)
            pos_i = pos_ref[rows, :1].astype(jnp.int32)
            perm_t = jnp.where(lax.broadcasted_iota(jnp.int32, (PERM_BLOCK, tm), 1) == pos_i,
                               1.0, 0.0).astype(BF16)
            out = h_ref[rows, :] + jnp.dot(perm_t, xs_ref[...], preferred_element_type=F32)
            if final_norm:
                out = _rms(out, gf_ref[...])
            o_ref[rows, :] = out


def _moe(h, g, wr, br, wg, wu, wd, gf, final_norm):
    n, d = h.shape
    tm = MOE_TILE
    tile = lambda i, e: (i, 0)
    const2 = lambda i, e: (0, 0)
    group = lambda i, e: (e, 0, 0)
    triu = jnp.tri(tm, tm, -1, dtype=BF16).T
    return pl.pallas_call(
        functools.partial(_moe_kernel, final_norm=final_norm),
        grid=(n // tm, N_GROUPS),
        in_specs=[
            pl.BlockSpec((tm, d), tile),
            pl.BlockSpec((1, d), const2),
            pl.BlockSpec(wr.shape, const2),
            pl.BlockSpec(br.shape, const2),
            pl.BlockSpec(triu.shape, const2),
            pl.BlockSpec((EXPERTS_PER_GROUP,) + wg.shape[1:], group),
            pl.BlockSpec((EXPERTS_PER_GROUP,) + wu.shape[1:], group),
            pl.BlockSpec((1,) + wd.shape[1:], group),
            pl.BlockSpec((1, d), const2),
        ],
        out_specs=pl.BlockSpec((tm, d), tile),
        out_shape=jax.ShapeDtypeStruct((n, d), F32),
        scratch_shapes=[
            pltpu.VMEM((tm, d), BF16),
            pltpu.VMEM((tm, ROUTER_LANES), F32),
            pltpu.VMEM((tm, d), F32),
            pltpu.VMEM((tm, ROUTER_LANES), F32),
            pltpu.VMEM((2, MOE_BLOCK, D_EXPERT), F32),
            pltpu.VMEM((2, MOE_BLOCK, D_EXPERT), F32),
            pltpu.VMEM((MOE_BLOCK, EXPERTS_PER_GROUP * D_EXPERT), BF16),
            pltpu.SMEM((8,), jnp.int32),
        ],
        compiler_params=pltpu.CompilerParams(
            dimension_semantics=("arbitrary", "arbitrary"), vmem_limit_bytes=VMEM_LIMIT),
        name="hier_moe",
    )(h, g, wr, br, triu, wg, wu, wd, gf)


def _router_params(wg_router, bg_router, we_router, be_router):
    d = wg_router.shape[0]
    we = jnp.transpose(we_router, (1, 0, 2)).reshape(d, N_EXPERTS)
    w = jnp.concatenate([wg_router, we], axis=1)
    w = jnp.pad(w, ((0, 0), (0, ROUTER_LANES - w.shape[1])))
    bias = jnp.concatenate([bg_router, be_router.reshape(N_EXPERTS)])
    bias = jnp.pad(bias, (0, ROUTER_LANES - bias.shape[0])).reshape(1, ROUTER_LANES)
    w_hi = w.astype(BF16)
    w_lo = (w - w_hi.astype(F32)).astype(BF16)
    return jnp.concatenate([w_hi, w_lo], axis=1), bias.astype(F32)


def _band_bias_t(rel_bias):
    width = PAIR_BAND + PAIR - 1
    m = jnp.arange(width + 1) - (PAIR - 1) - LEFT
    ext = rel_bias.astype(F32)[:, jnp.clip(m, -MAX_REL, MAX_REL) + MAX_REL]
    skew = jnp.tile(ext, (1, PAIR))[:, :PAIR * width].reshape(ATT_HEADS, PAIR, width)
    bias = skew[:, :, PAIR - 1:PAIR - 1 + PAIR_BAND]
    first = (jnp.arange(PAIR) // CHUNK * CHUNK)[:, None]
    kk = jnp.arange(PAIR_BAND)[None, :]
    bias = jnp.where(((kk >= first) & (kk < first + BAND))[None], bias * LOG2E, -1e30)
    bias_t = jnp.transpose(bias, (0, 2, 1)).reshape(ATT_HEADS // 2, 2, PAIR_BAND, PAIR)
    return jnp.transpose(bias_t, (0, 2, 1, 3)).reshape(ATT_HEADS // 2, PAIR_BAND, 2 * PAIR)


def kernel(x, norm_mix_g, norm_ffn_g, final_norm_g, ab_w_in, pool_w, pool_scale, att_rel_bias,
           ab_w_out, sgu_w_in, sgu_b_in, sgu_ln_g, sgu_ln_b, sgu_w_s, sgu_b_s, sgu_w_out,
           moe_wg_router, moe_bg_router, moe_we_router, moe_be_router,
           moe_w_gate, moe_w_up, moe_w_down):
    b, s, d = x.shape
    row = lambda a: a.reshape(1, -1).astype(F32)

    def moe(h, layer, final_norm):
        wr, br = _router_params(moe_wg_router[layer], moe_bg_router[layer],
                                moe_we_router[layer], moe_be_router[layer])
        wd = moe_w_down[layer].astype(BF16).reshape(N_GROUPS, EXPERTS_PER_GROUP * D_EXPERT, d)
        return _moe(h, row(norm_ffn_g[layer]), wr, br, moe_w_gate[layer].astype(BF16),
                    moe_w_up[layer].astype(BF16), wd, row(final_norm_g), final_norm)

    w_in = ab_w_in[0]
    n_pqk = POOL_WIDTH + 2 * ATT_WIDTH
    p, q, k, vt = _inproj(x, row(norm_mix_g[0]), w_in[:, :n_pqk].astype(BF16),
                          w_in[:, n_pqk:].T.astype(BF16))
    h = _mix(x, p, q, k, vt, _band_bias_t(att_rel_bias[0]), pool_w[0].astype(BF16),
             row(pool_scale[0]), ab_w_out[0].astype(BF16))
    h = moe(h.reshape(b * s, d), 0, False)

    h = _sgu(h, row(norm_mix_g[1]), sgu_w_in[0].astype(BF16), row(sgu_b_in[0]),
             row(sgu_ln_g[0]), row(sgu_ln_b[0]), sgu_w_s[0].astype(BF16),
             sgu_b_s[0].astype(F32)[:, :, None], sgu_w_out[0].astype(BF16))
    h = moe(h, 1, True)
    return h.reshape(b, s, d)
```

```python
import functools

import jax
import jax.numpy as jnp
from jax import lax
from jax.experimental import pallas as pl
from jax.experimental.pallas import tpu as pltpu

EPS = 1e-6
CHUNK = 64
POOL_WINDOWS = (2, 4, 8, 16)
POOL_GROUP_DIM = 128
POOL_WIDTH = 512
POOL_HALO = 16
ATT_HEADS = 8
ATT_HEAD_DIM = 64
ATT_WIDTH = 512
LEFT_CHUNKS = 8
LEFT = LEFT_CHUNKS * CHUNK
BAND = LEFT + CHUNK
PAIR = 2 * CHUNK
PAIR_BAND = LEFT + PAIR
MAX_REL = 128
ATT_SCALE = ATT_HEAD_DIM ** -0.5
LOG2E = 1.4426950408889634
VT_ONES = 16
VT_PAIR = 2 * ATT_HEAD_DIM + VT_ONES
VT_ROWS = (ATT_HEADS // 2) * VT_PAIR
SGU_CHUNK = 128
SGU_HEADS = 8
N_GROUPS = 4
EXPERTS_PER_GROUP = 4
N_EXPERTS = 16
D_EXPERT = 256
ROUTER_LANES = 128
MOE_TILE = 1024
MOE_BLOCK = 320
MOE_GROUPS_PER_STEP = 2
PERM_BLOCK = 256
ROW_ALIGN = 16

VMEM_LIMIT = 56 * 1024 * 1024

F32 = jnp.float32
BF16 = jnp.bfloat16


def _rms(x, g):
    return x * lax.rsqrt(jnp.mean(x * x, axis=-1, keepdims=True) + EPS) * g


def _inproj_kernel(x_ref, g_ref, w_ref, wvt_ref, p_ref, q_ref, k_ref, vt_ref):
    i = pl.program_id(1)

    @pl.when(i == 0)
    def _():
        k_ref[...] = jnp.zeros_like(k_ref)
        vt_ref[...] = jnp.zeros_like(vt_ref)

    @pl.when(i > 0)
    def _():
        xn = _rms(x_ref[0], g_ref[...]).astype(BF16)
        z = jnp.dot(xn, w_ref[...], preferred_element_type=F32)
        p_ref[0] = z[:, :POOL_WIDTH]
        q_ref[0] = (z[:, POOL_WIDTH:POOL_WIDTH + ATT_WIDTH] * (ATT_SCALE * LOG2E)).astype(BF16)
        k_ref[0] = z[:, POOL_WIDTH + ATT_WIDTH:].astype(BF16)
        vt = lax.dot_general(wvt_ref[...], xn, (((1,), (1,)), ((), ())), preferred_element_type=F32)
        for j in range(ATT_HEADS // 2):
            rows = slice(2 * j * ATT_HEAD_DIM, 2 * (j + 1) * ATT_HEAD_DIM)
            vt_ref[0, j * VT_PAIR:j * VT_PAIR + 2 * ATT_HEAD_DIM, :] = vt[rows].astype(BF16)
            vt_ref[0, j * VT_PAIR + 2 * ATT_HEAD_DIM:(j + 1) * VT_PAIR, :] = jnp.ones(
                (VT_ONES, vt.shape[1]), BF16)


def _inproj(x, g, w_pqk, w_vt):
    b, s, d = x.shape
    tm = LEFT
    nt = s // tm
    prev = lambda bi, i: (bi, jnp.maximum(i - 1, 0), 0)
    const2 = lambda bi, i: (0, 0)
    return pl.pallas_call(
        _inproj_kernel,
        grid=(b, nt + 1),
        in_specs=[
            pl.BlockSpec((1, tm, d), prev),
            pl.BlockSpec((1, d), const2),
            pl.BlockSpec(w_pqk.shape, const2),
            pl.BlockSpec(w_vt.shape, const2),
        ],
        out_specs=[
            pl.BlockSpec((1, tm, POOL_WIDTH), prev),
            pl.BlockSpec((1, tm, ATT_WIDTH), prev),
            pl.BlockSpec((1, tm, ATT_WIDTH), lambda bi, i: (bi, i, 0)),
            pl.BlockSpec((1, VT_ROWS, tm), lambda bi, i: (bi, 0, i)),
        ],
        out_shape=[
            jax.ShapeDtypeStruct((b, s, POOL_WIDTH), F32),
            jax.ShapeDtypeStruct((b, s, ATT_WIDTH), BF16),
            jax.ShapeDtypeStruct((b, s + LEFT, ATT_WIDTH), BF16),
            jax.ShapeDtypeStruct((b, VT_ROWS, s + LEFT), BF16),
        ],
        compiler_params=pltpu.CompilerParams(
            dimension_semantics=("arbitrary", "arbitrary"), vmem_limit_bytes=VMEM_LIMIT),
        name="inproj",
    )(x, g, w_pqk, w_vt)


def _attend_tile(q_ref, k_ref, vt_ref, biast_ref, mix, st_buf, i, tq, masked, rows_done):
    lane = lax.broadcasted_iota(jnp.int32, (PAIR, 2 * ATT_HEAD_DIM), 1)
    even = lane < ATT_HEAD_DIM
    zero = jnp.zeros((), BF16)
    units = [(cp, j) for cp in range(tq // PAIR) for j in range(ATT_HEADS // 2)]

    def band_base(cp):
        return pl.multiple_of((i * (tq // PAIR) + cp) * PAIR, PAIR)

    def scores(u):
        cp, j = units[u]
        cols = slice(2 * j * ATT_HEAD_DIM, 2 * (j + 1) * ATT_HEAD_DIM)
        q2 = q_ref[0, cp * PAIR:(cp + 1) * PAIR, cols]
        qs = jnp.concatenate([jnp.where(even, q2, zero), jnp.where(even, zero, q2)], axis=0)
        kb = k_ref[0, pl.ds(band_base(cp), PAIR_BAND), cols]
        st_buf[u % 2] = lax.dot_general(kb, qs, (((1,), (1,)), ((), ())),
                                        preferred_element_type=F32)

    scores(0)
    for u, (cp, j) in enumerate(units):
        if u + 1 < len(units):
            scores(u + 1)
        cols = slice(2 * j * ATT_HEAD_DIM, 2 * (j + 1) * ATT_HEAD_DIM)
        base = band_base(cp)
        st = st_buf[u % 2] + biast_ref[j]
        if masked:
            valid = (lax.broadcasted_iota(jnp.int32, (PAIR_BAND, 1), 0) + base - LEFT) >= 0
            st = jnp.where(valid, st, -1e30)
        m = jnp.max(st, axis=0, keepdims=True)
        e = jnp.exp2(st - m).astype(BF16)
        vtb = vt_ref[0, j * VT_PAIR:(j + 1) * VT_PAIR, pl.ds(base, PAIR_BAND)]
        ot = jnp.dot(vtb, e, preferred_element_type=F32)
        ot = ot[:2 * ATT_HEAD_DIM] * (1.0 / ot[2 * ATT_HEAD_DIM:2 * ATT_HEAD_DIM + 1])
        o2t = jnp.concatenate([ot[:ATT_HEAD_DIM, :PAIR], ot[ATT_HEAD_DIM:, PAIR:]], axis=0)
        mix[cp * PAIR:(cp + 1) * PAIR, POOL_WIDTH + 2 * j * ATT_HEAD_DIM:
            POOL_WIDTH + 2 * (j + 1) * ATT_HEAD_DIM] = o2t.T.astype(BF16)
        if j == ATT_HEADS // 2 - 1:
            rows_done(cp)


def _mix_kernel(x_ref, p_ref, halo_ref, q_ref, k_ref, vt_ref, biast_ref, wpool_ref, pscale_ref,
                wout_ref, o_ref, pbuf, mix, st_buf, *, tq):
    i = pl.program_id(1)

    def pool():
        halo = halo_ref[0]
        pbuf[0:POOL_HALO, :] = jnp.where(i == 0, jnp.zeros_like(halo), halo)
        pbuf[POOL_HALO:, :] = p_ref[0]
        t_abs = i * tq + lax.broadcasted_iota(jnp.int32, (tq, 1), 0)
        for gi, w in enumerate(POOL_WINDOWS):
            cols = slice(gi * POOL_GROUP_DIM, (gi + 1) * POOL_GROUP_DIM)
            cur = pbuf[POOL_HALO:, cols]
            acc = pbuf[:, cols]
            step = 1
            while step < w:
                acc = acc + pltpu.roll(acc, step, 0)
                step *= 2
            cnt = jnp.minimum(t_abs + 1, w).astype(F32)
            pooled = (acc[POOL_HALO:] / cnt - cur).astype(BF16)
            y = jnp.dot(pooled, wpool_ref[gi], preferred_element_type=F32) * pscale_ref[:, cols]
            mix[:, cols] = y.astype(BF16)

    def out_proj(cp):
        if cp % 2 == 1:
            rows = slice((cp - 1) * PAIR, (cp + 1) * PAIR)
            o_ref[0, rows, :] = x_ref[0, rows, :] + jnp.dot(mix[rows, :], wout_ref[...],
                                                          preferred_element_type=F32)

    def step(masked):
        pool()
        _attend_tile(q_ref, k_ref, vt_ref, biast_ref, mix, st_buf, i, tq, masked, out_proj)

    pl.when(i * tq < LEFT)(lambda: step(True))
    pl.when(i * tq >= LEFT)(lambda: step(False))


def _mix(x, p, q, k, vt, biast, w_pool, pool_scale, w_out):
    b, s, d = x.shape
    tq = 512
    halo_blocks = tq // POOL_HALO
    tile = lambda bi, i: (bi, i, 0)
    full = lambda bi, i: (bi, 0, 0)
    const2 = lambda bi, i: (0, 0)
    const3 = lambda bi, i: (0, 0, 0)
    return pl.pallas_call(
        functools.partial(_mix_kernel, tq=tq),
        grid=(b, s // tq),
        in_specs=[
            pl.BlockSpec((1, tq, d), tile),
            pl.BlockSpec((1, tq, POOL_WIDTH), tile),
            pl.BlockSpec((1, POOL_HALO, POOL_WIDTH),
                         lambda bi, i: (bi, jnp.maximum(i * halo_blocks - 1, 0), 0)),
            pl.BlockSpec((1, tq, ATT_WIDTH), tile),
            pl.BlockSpec((1, s + LEFT, ATT_WIDTH), full),
            pl.BlockSpec((1, VT_ROWS, s + LEFT), full),
            pl.BlockSpec(biast.shape, const3),
            pl.BlockSpec(w_pool.shape, const3),
            pl.BlockSpec(pool_scale.shape, const2),
            pl.BlockSpec(w_out.shape, const2),
        ],
        out_specs=pl.BlockSpec((1, tq, d), tile),
        out_shape=jax.ShapeDtypeStruct((b, s, d), F32),
        scratch_shapes=[
            pltpu.VMEM((POOL_HALO + tq, POOL_WIDTH), F32),
            pltpu.VMEM((tq, POOL_WIDTH + ATT_WIDTH), BF16),
            pltpu.VMEM((2, PAIR_BAND, 2 * PAIR), F32),
        ],
        compiler_params=pltpu.CompilerParams(
            dimension_semantics=("arbitrary", "arbitrary"), vmem_limit_bytes=VMEM_LIMIT),
        name="pool_attn_mix",
    )(x, p, p, q, k, vt, biast, w_pool, pool_scale, w_out)


def _sgu_kernel(h_ref, g_ref, win_ref, bin_ref, lng_ref, lnb_ref, ws_ref, bs_ref, wout_ref,
                o_ref, xn_ref, z_buf, gated, *, tm, sub):
    d = h_ref.shape[-1]
    hd = d // SGU_HEADS
    xn_ref[...] = _rms(h_ref[...], g_ref[...]).astype(BF16)
    row = lax.broadcasted_iota(jnp.int32, (SGU_CHUNK, SGU_CHUNK), 0)
    col = lax.broadcasted_iota(jnp.int32, (SGU_CHUNK, SGU_CHUNK), 1)
    ws = [jnp.where(row >= col, ws_ref[hh], jnp.zeros((), BF16)) for hh in range(SGU_HEADS)]

    def in_proj(s):
        z_buf[s % 2] = jnp.dot(xn_ref[s * sub:(s + 1) * sub, :], win_ref[...],
                               preferred_element_type=F32)

    in_proj(0)
    for s in range(tm // sub):
        if s + 1 < tm // sub:
            in_proj(s + 1)
        z = z_buf[s % 2] + bin_ref[...]
        z = 0.5 * z * (1.0 + lax.erf(z * (2.0 ** -0.5)))
        u = z[:, :d]
        v = z[:, d:]
        mu = jnp.mean(v, axis=-1, keepdims=True)
        vc = v - mu
        var = jnp.mean(vc * vc, axis=-1, keepdims=True)
        vn = (vc * lax.rsqrt(var + EPS) * lng_ref[...] + lnb_ref[...]).astype(BF16)
        for hh in range(SGU_HEADS):
            cols = slice(hh * hd, (hh + 1) * hd)
            for n in range(sub // SGU_CHUNK):
                rows = slice(n * SGU_CHUNK, (n + 1) * SGU_CHUNK)
                mixed = jnp.dot(ws[hh], vn[rows, cols], preferred_element_type=F32) + bs_ref[hh]
                gated[s * sub + n * SGU_CHUNK:s * sub + (n + 1) * SGU_CHUNK, cols] = (
                    u[rows, cols] * mixed).astype(BF16)
        blk = slice(s * sub, (s + 1) * sub)
        o_ref[blk, :] = h_ref[blk, :] + jnp.dot(gated[blk, :], wout_ref[...],
                                                preferred_element_type=F32)


def _sgu(h, g, w_in, b_in, ln_g, ln_b, w_s, b_s, w_out):
    n, d = h.shape
    tm, sub = 1024, 256
    tile = lambda i: (i, 0)
    const2 = lambda i: (0, 0)
    const3 = lambda i: (0, 0, 0)
    return pl.pallas_call(
        functools.partial(_sgu_kernel, tm=tm, sub=sub),
        grid=(n // tm,),
        in_specs=[
            pl.BlockSpec((tm, d), tile),
            pl.BlockSpec((1, d), const2),
            pl.BlockSpec(w_in.shape, const2),
            pl.BlockSpec(b_in.shape, const2),
            pl.BlockSpec(ln_g.shape, const2),
            pl.BlockSpec(ln_b.shape, const2),
            pl.BlockSpec(w_s.shape, const3),
            pl.BlockSpec(b_s.shape, const3),
            pl.BlockSpec(w_out.shape, const2),
        ],
        out_specs=pl.BlockSpec((tm, d), tile),
        out_shape=jax.ShapeDtypeStruct((n, d), F32),
        scratch_shapes=[
            pltpu.VMEM((tm, d), BF16),
            pltpu.VMEM((2, sub, 2 * d), F32),
            pltpu.VMEM((tm, d), BF16),
        ],
        compiler_params=pltpu.CompilerParams(
            dimension_semantics=("arbitrary",), vmem_limit_bytes=VMEM_LIMIT),
        name="sgu",
    )(h, g, w_in, b_in, ln_g, ln_b, w_s, b_s, w_out)


def _route_rows(lt):
    neg = jnp.float32(-jnp.inf)

    def first_max(rows):
        m = functools.reduce(jnp.maximum, rows)
        idx = jnp.int32(len(rows))
        for r in reversed(range(len(rows))):
            idx = jnp.where(rows[r] == m, jnp.int32(r), idx)
        return m, idx

    gl = [lt[g:g + 1, :] for g in range(N_GROUPS)]
    gmax, gidx = first_max(gl)
    g_w = 1.0 / functools.reduce(jnp.add, [jnp.exp(x - gmax) for x in gl])
    el = []
    for j in range(EXPERTS_PER_GROUP):
        row = lt[N_GROUPS + j:N_GROUPS + j + 1, :]
        for g in range(1, N_GROUPS):
            r = N_GROUPS + EXPERTS_PER_GROUP * g + j
            row = jnp.where(gidx == g, lt[r:r + 1, :], row)
        el.append(row)
    m1, i1 = first_max(el)
    m2, i2 = first_max([jnp.where(i1 == j, neg, el[j]) for j in range(EXPERTS_PER_GROUP)])
    e2 = jnp.exp(m2 - m1)
    den = 1.0 + e2
    w1 = g_w / den
    w2 = g_w * e2 / den
    cw = [jnp.where(i1 == j, w1, 0.0) + jnp.where(i2 == j, w2, 0.0) for j in range(EXPERTS_PER_GROUP)]
    return gidx, cw


def _moe_kernel(h_ref, g_ref, wr_ref, br_ref, triu_ref, wg_ref, wu_ref, wd_ref, gf_ref,
                o_ref, xs_ref, ws_ref, ys_ref, pos_ref, hg_buf, hu_buf, act_ref, bounds, *, final_norm):
    step = pl.program_id(1)
    tm, d = h_ref.shape
    nperm = tm // PERM_BLOCK

    @pl.when(step == 0)
    def _():
        xn = _rms(h_ref[...], g_ref[...])
        x_hi = xn.astype(BF16)
        x_lo = (xn - x_hi.astype(F32)).astype(BF16)
        both = jnp.dot(x_hi, wr_ref[...], preferred_element_type=F32)
        logits = (both[:, :ROUTER_LANES] + both[:, ROUTER_LANES:]
                  + jnp.dot(x_lo, wr_ref[:, :ROUTER_LANES], preferred_element_type=F32)) + br_ref[...]
        gidx, cw = _route_rows(logits.T)

        row16 = lax.broadcasted_iota(jnp.int32, (ROW_ALIGN, tm), 0)
        onehot = jnp.where(row16 == gidx, 1.0, 0.0)
        rank = jnp.dot(onehot.astype(BF16), triu_ref[...], preferred_element_type=F32)
        pos_row = jnp.zeros((1, tm), F32)
        start = jnp.zeros((1, 1), F32)
        for g in range(N_GROUPS):
            bounds[g] = start[0, 0].astype(jnp.int32)
            mine = gidx == g
            pos_row = jnp.where(mine, rank[g:g + 1, :] + start, pos_row)
            start = start + jnp.sum(jnp.where(mine, 1.0, 0.0), axis=-1, keepdims=True)
        bounds[N_GROUPS] = jnp.int32(tm)

        packed = jnp.where(row16 == 0, pos_row, 0.0)
        for j in range(EXPERTS_PER_GROUP):
            piece = cw[j]
            for p in range(3):
                part = piece.astype(BF16).astype(F32)
                packed = packed + jnp.where(row16 == 1 + p * EXPERTS_PER_GROUP + j, part, 0.0)
                piece = piece - part
        packed = jnp.concatenate(
            [packed, jnp.zeros((ROUTER_LANES - ROW_ALIGN, tm), F32)], axis=0)
        tok = packed.T
        pos_ref[...] = tok
        w3 = tok.astype(BF16)
        pos_row = pos_row.astype(jnp.int32)

        for rb in range(nperm):
            rows = slice(rb * PERM_BLOCK, (rb + 1) * PERM_BLOCK)
            sorted_row = lax.broadcasted_iota(jnp.int32, (PERM_BLOCK, tm), 0) + rb * PERM_BLOCK
            perm = jnp.where(sorted_row == pos_row, 1.0, 0.0).astype(BF16)
            xs_ref[rows, :] = jnp.dot(perm, x_hi, preferred_element_type=F32).astype(BF16)
            ws_ref[rows, :] = jnp.dot(perm, w3, preferred_element_type=F32)
        ys_ref[...] = jnp.zeros_like(ys_ref)

    lane_b = lax.broadcasted_iota(jnp.int32, (MOE_BLOCK, ROUTER_LANES), 1)

    def expert_block(gg, seg_lo, seg_hi, blk_lo):
        row0 = pl.multiple_of(jnp.minimum(blk_lo, tm - MOE_BLOCK), ROW_ALIGN)
        rows = pl.ds(row0, MOE_BLOCK)
        sorted_row = lax.broadcasted_iota(jnp.int32, (MOE_BLOCK, 1), 0) + row0
        in_seg = ((sorted_row >= jnp.maximum(seg_lo, blk_lo))
                  & (sorted_row < jnp.minimum(seg_hi, blk_lo + MOE_BLOCK)))
        w3 = ws_ref[rows, :]

        def gate_up(j):
            e = gg * EXPERTS_PER_GROUP + j
            hg_buf[j % 2] = jnp.dot(xs_ref[rows, :], wg_ref[e], preferred_element_type=F32)
            hu_buf[j % 2] = jnp.dot(xs_ref[rows, :], wu_ref[e], preferred_element_type=F32)

        gate_up(0)
        for j in range(EXPERTS_PER_GROUP):
            if j + 1 < EXPERTS_PER_GROUP:
                gate_up(j + 1)
            mine = (((lane_b - 1) % EXPERTS_PER_GROUP == j) & (lane_b >= 1)
                    & (lane_b <= 3 * EXPERTS_PER_GROUP))
            c = jnp.sum(jnp.where(mine, w3, 0.0), axis=-1, keepdims=True)
            c = jnp.where(in_seg, c, 0.0)
            hg = hg_buf[j % 2]
            act = hg * (1.0 / (1.0 + jnp.exp(-hg))) * hu_buf[j % 2] * c
            act_ref[:, j * D_EXPERT:(j + 1) * D_EXPERT] = act.astype(BF16)
        ys_ref[rows, :] += jnp.dot(act_ref[...], wd_ref[gg], preferred_element_type=F32)

    for gg in range(MOE_GROUPS_PER_STEP):
        grp = step * MOE_GROUPS_PER_STEP + gg
        seg_lo = bounds[grp]
        seg_hi = bounds[grp + 1]
        first = (seg_lo // ROW_ALIGN) * ROW_ALIGN
        for k in range(pl.cdiv(tm, MOE_BLOCK)):
            blk_lo = first + k * MOE_BLOCK
            pl.when(blk_lo < seg_hi)(functools.partial(expert_block, gg, seg_lo, seg_hi, blk_lo))

    @pl.when(step == N_GROUPS // MOE_GROUPS_PER_STEP - 1)
    def _():
        xs_ref[...] = ys_ref[...].astype(BF16)
        for tb in range(nperm):
            rows = slice(tb * PERM_BLOCK, (tb + 1) * PERM_BLOCK)
            pos_i = pos_ref[rows, :1].astype(jnp.int32)
            perm_t = jnp.where(lax.broadcasted_iota(jnp.int32, (PERM_BLOCK, tm), 1) == pos_i,
                               1.0, 0.0).astype(BF16)
            out = h_ref[rows, :] + jnp.dot(perm_t, xs_ref[...], preferred_element_type=F32)
            if final_norm:
                out = _rms(out, gf_ref[...])
            o_ref[rows, :] = out


def _moe(h, g, wr, br, wg, wu, wd, gf, final_norm):
    n, d = h.shape
    tm = MOE_TILE
    tile = lambda i, e: (i, 0)
    const2 = lambda i, e: (0, 0)
    group = lambda i, e: (e, 0, 0)
    triu = jnp.tri(tm, tm, -1, dtype=BF16).T
    return pl.pallas_call(
        functools.partial(_moe_kernel, final_norm=final_norm),
        grid=(n // tm, N_GROUPS // MOE_GROUPS_PER_STEP),
        in_specs=[
            pl.BlockSpec((tm, d), tile),
            pl.BlockSpec((1, d), const2),
            pl.BlockSpec(wr.shape, const2),
            pl.BlockSpec(br.shape, const2),
            pl.BlockSpec(triu.shape, const2),
            pl.BlockSpec((MOE_GROUPS_PER_STEP * EXPERTS_PER_GROUP,) + wg.shape[1:], group),
            pl.BlockSpec((MOE_GROUPS_PER_STEP * EXPERTS_PER_GROUP,) + wu.shape[1:], group),
            pl.BlockSpec((MOE_GROUPS_PER_STEP,) + wd.shape[1:], group),
            pl.BlockSpec((1, d), const2),
        ],
        out_specs=pl.BlockSpec((tm, d), tile),
        out_shape=jax.ShapeDtypeStruct((n, d), F32),
        scratch_shapes=[
            pltpu.VMEM((tm, d), BF16),
            pltpu.VMEM((tm, ROUTER_LANES), F32),
            pltpu.VMEM((tm, d), F32),
            pltpu.VMEM((tm, ROUTER_LANES), F32),
            pltpu.VMEM((2, MOE_BLOCK, D_EXPERT), F32),
            pltpu.VMEM((2, MOE_BLOCK, D_EXPERT), F32),
            pltpu.VMEM((MOE_BLOCK, EXPERTS_PER_GROUP * D_EXPERT), BF16),
            pltpu.SMEM((8,), jnp.int32),
        ],
        compiler_params=pltpu.CompilerParams(
            dimension_semantics=("arbitrary", "arbitrary"), vmem_limit_bytes=VMEM_LIMIT),
        name="hier_moe",
    )(h, g, wr, br, triu, wg, wu, wd, gf)


def _router_params(wg_router, bg_router, we_router, be_router):
    d = wg_router.shape[0]
    we = jnp.transpose(we_router, (1, 0, 2)).reshape(d, N_EXPERTS)
    w = jnp.concatenate([wg_router, we], axis=1)
    w = jnp.pad(w, ((0, 0), (0, ROUTER_LANES - w.shape[1])))
    bias = jnp.concatenate([bg_router, be_router.reshape(N_EXPERTS)])
    bias = jnp.pad(bias, (0, ROUTER_LANES - bias.shape[0])).reshape(1, ROUTER_LANES)
    w_hi = w.astype(BF16)
    w_lo = (w - w_hi.astype(F32)).astype(BF16)
    return jnp.concatenate([w_hi, w_lo], axis=1), bias.astype(F32)


def _band_bias_t(rel_bias):
    width = PAIR_BAND + PAIR - 1
    m = jnp.arange(width + 1) - (PAIR - 1) - LEFT
    ext = rel_bias.astype(F32)[:, jnp.clip(m, -MAX_REL, MAX_REL) + MAX_REL]
    skew = jnp.tile(ext, (1, PAIR))[:, :PAIR * width].reshape(ATT_HEADS, PAIR, width)
    bias = skew[:, :, PAIR - 1:PAIR - 1 + PAIR_BAND]
    first = (jnp.arange(PAIR) // CHUNK * CHUNK)[:, None]
    kk = jnp.arange(PAIR_BAND)[None, :]
    bias = jnp.where(((kk >= first) & (kk < first + BAND))[None], bias * LOG2E, -1e30)
    bias_t = jnp.transpose(bias, (0, 2, 1)).reshape(ATT_HEADS // 2, 2, PAIR_BAND, PAIR)
    return jnp.transpose(bias_t, (0, 2, 1, 3)).reshape(ATT_HEADS // 2, PAIR_BAND, 2 * PAIR)


def kernel(x, norm_mix_g, norm_ffn_g, final_norm_g, ab_w_in, pool_w, pool_scale, att_rel_bias,
           ab_w_out, sgu_w_in, sgu_b_in, sgu_ln_g, sgu_ln_b, sgu_w_s, sgu_b_s, sgu_w_out,
           moe_wg_router, moe_bg_router, moe_we_router, moe_be_router,
           moe_w_gate, moe_w_up, moe_w_down):
    b, s, d = x.shape
    row = lambda a: a.reshape(1, -1).astype(F32)

    def moe(h, layer, final_norm):
        wr, br = _router_params(moe_wg_router[layer], moe_bg_router[layer],
                                moe_we_router[layer], moe_be_router[layer])
        wd = moe_w_down[layer].astype(BF16).reshape(N_GROUPS, EXPERTS_PER_GROUP * D_EXPERT, d)
        return _moe(h, row(norm_ffn_g[layer]), wr, br, moe_w_gate[layer].astype(BF16),
                    moe_w_up[layer].astype(BF16), wd, row(final_norm_g), final_norm)

    w_in = ab_w_in[0]
    n_pqk = POOL_WIDTH + 2 * ATT_WIDTH
    p, q, k, vt = _inproj(x, row(norm_mix_g[0]), w_in[:, :n_pqk].astype(BF16),
                          w_in[:, n_pqk:].T.astype(BF16))
    h = _mix(x, p, q, k, vt, _band_bias_t(att_rel_bias[0]), pool_w[0].astype(BF16),
             row(pool_scale[0]), ab_w_out[0].astype(BF16))
    h = moe(h.reshape(b * s, d), 0, False)

    h = _sgu(h, row(norm_mix_g[1]), sgu_w_in[0].astype(BF16), row(sgu_b_in[0]),
             row(sgu_ln_g[0]), row(sgu_ln_b[0]), sgu_w_s[0].astype(BF16),
             sgu_b_s[0].astype(F32)[:, :, None], sgu_w_out[0].astype(BF16))
    h = moe(h, 1, True)
    return h.reshape(b, s, d)
```

```python
import functools

import jax
import jax.numpy as jnp
from jax import lax
from jax.experimental import pallas as pl
from jax.experimental.pallas import tpu as pltpu

EPS = 1e-6
CHUNK = 64
POOL_WINDOWS = (2, 4, 8, 16)
POOL_GROUP_DIM = 128
POOL_WIDTH = 512
POOL_HALO = 16
ATT_HEADS = 8
ATT_HEAD_DIM = 64
ATT_WIDTH = 512
LEFT_CHUNKS = 8
LEFT = LEFT_CHUNKS * CHUNK
BAND = LEFT + CHUNK
PAIR = 2 * CHUNK
PAIR_BAND = LEFT + PAIR
MAX_REL = 128
ATT_SCALE = ATT_HEAD_DIM ** -0.5
LOG2E = 1.4426950408889634
VT_ONES = 16
VT_PAIR = 2 * ATT_HEAD_DIM + VT_ONES
VT_ROWS = (ATT_HEADS // 2) * VT_PAIR
SGU_CHUNK = 128
SGU_HEADS = 8
N_GROUPS = 4
EXPERTS_PER_GROUP = 4
N_EXPERTS = 16
D_EXPERT = 256
ROUTER_LANES = 128
MOE_TILE = 1024
MOE_BLOCK = 320
MOE_GROUPS_PER_STEP = 2
PERM_BLOCK = 256
ROW_ALIGN = 16

VMEM_LIMIT = 56 * 1024 * 1024

F32 = jnp.float32
BF16 = jnp.bfloat16


def _rms(x, g):
    return x * lax.rsqrt(jnp.mean(x * x, axis=-1, keepdims=True) + EPS) * g


def _inproj_kernel(x_ref, g_ref, w_ref, wvt_ref, p_ref, q_ref, k_ref, vt_ref):
    i = pl.program_id(1)

    @pl.when(i == 0)
    def _():
        k_ref[...] = jnp.zeros_like(k_ref)
        vt_ref[...] = jnp.zeros_like(vt_ref)

    @pl.when(i > 0)
    def _():
        xn = _rms(x_ref[0], g_ref[...]).astype(BF16)
        z = jnp.dot(xn, w_ref[...], preferred_element_type=F32)
        p_ref[0] = z[:, :POOL_WIDTH]
        q_ref[0] = (z[:, POOL_WIDTH:POOL_WIDTH + ATT_WIDTH] * (ATT_SCALE * LOG2E)).astype(BF16)
        k_ref[0] = z[:, POOL_WIDTH + ATT_WIDTH:].astype(BF16)
        vt = lax.dot_general(wvt_ref[...], xn, (((1,), (1,)), ((), ())), preferred_element_type=F32)
        for j in range(ATT_HEADS // 2):
            rows = slice(2 * j * ATT_HEAD_DIM, 2 * (j + 1) * ATT_HEAD_DIM)
            vt_ref[0, j * VT_PAIR:j * VT_PAIR + 2 * ATT_HEAD_DIM, :] = vt[rows].astype(BF16)
            vt_ref[0, j * VT_PAIR + 2 * ATT_HEAD_DIM:(j + 1) * VT_PAIR, :] = jnp.ones(
                (VT_ONES, vt.shape[1]), BF16)


def _inproj(x, g, w_pqk, w_vt):
    b, s, d = x.shape
    tm = LEFT
    nt = s // tm
    prev = lambda bi, i: (bi, jnp.maximum(i - 1, 0), 0)
    const2 = lambda bi, i: (0, 0)
    return pl.pallas_call(
        _inproj_kernel,
        grid=(b, nt + 1),
        in_specs=[
            pl.BlockSpec((1, tm, d), prev),
            pl.BlockSpec((1, d), const2),
            pl.BlockSpec(w_pqk.shape, const2),
            pl.BlockSpec(w_vt.shape, const2),
        ],
        out_specs=[
            pl.BlockSpec((1, tm, POOL_WIDTH), prev),
            pl.BlockSpec((1, tm, ATT_WIDTH), prev),
            pl.BlockSpec((1, tm, ATT_WIDTH), lambda bi, i: (bi, i, 0)),
            pl.BlockSpec((1, VT_ROWS, tm), lambda bi, i: (bi, 0, i)),
        ],
        out_shape=[
            jax.ShapeDtypeStruct((b, s, POOL_WIDTH), F32),
            jax.ShapeDtypeStruct((b, s, ATT_WIDTH), BF16),
            jax.ShapeDtypeStruct((b, s + LEFT, ATT_WIDTH), BF16),
            jax.ShapeDtypeStruct((b, VT_ROWS, s + LEFT), BF16),
        ],
        compiler_params=pltpu.CompilerParams(
            dimension_semantics=("arbitrary", "arbitrary"), vmem_limit_bytes=VMEM_LIMIT),
        name="inproj",
    )(x, g, w_pqk, w_vt)


def _attend_tile(q_ref, k_ref, vt_ref, biast_ref, mix, st_buf, i, tq, masked, rows_done):
    quad = 4 * ATT_HEAD_DIM
    lane = lax.broadcasted_iota(jnp.int32, (PAIR, quad), 1)
    zero = jnp.zeros((), BF16)
    units = [(cp, jj) for cp in range(tq // PAIR) for jj in range(ATT_HEADS // 4)]

    def band_base(cp):
        return pl.multiple_of((i * (tq // PAIR) + cp) * PAIR, PAIR)

    def scores(u):
        cp, jj = units[u]
        cols = slice(jj * quad, (jj + 1) * quad)
        q4 = q_ref[0, cp * PAIR:(cp + 1) * PAIR, cols]
        qs = jnp.concatenate(
            [jnp.where((lane >= h * ATT_HEAD_DIM) & (lane < (h + 1) * ATT_HEAD_DIM), q4, zero)
             for h in range(4)], axis=0)
        kb = k_ref[0, pl.ds(band_base(cp), PAIR_BAND), cols]
        st_buf[u % 2] = lax.dot_general(kb, qs, (((1,), (1,)), ((), ())),
                                        preferred_element_type=F32)

    scores(0)
    for u, (cp, jj) in enumerate(units):
        if u + 1 < len(units):
            scores(u + 1)
        base = band_base(cp)
        if masked:
            valid = (lax.broadcasted_iota(jnp.int32, (PAIR_BAND, 1), 0) + base - LEFT) >= 0
        for j in (2 * jj, 2 * jj + 1):
            st = st_buf[u % 2, :, (j % 2) * 2 * PAIR:(j % 2 + 1) * 2 * PAIR] + biast_ref[j]
            if masked:
                st = jnp.where(valid, st, -1e30)
            m = jnp.max(st, axis=0, keepdims=True)
            e = jnp.exp2(st - m).astype(BF16)
            vtb = vt_ref[0, j * VT_PAIR:(j + 1) * VT_PAIR, pl.ds(base, PAIR_BAND)]
            ot = jnp.dot(vtb, e, preferred_element_type=F32)
            ot = ot[:2 * ATT_HEAD_DIM] * (1.0 / ot[2 * ATT_HEAD_DIM:2 * ATT_HEAD_DIM + 1])
            o2t = jnp.concatenate([ot[:ATT_HEAD_DIM, :PAIR], ot[ATT_HEAD_DIM:, PAIR:]], axis=0)
            mix[cp * PAIR:(cp + 1) * PAIR, POOL_WIDTH + 2 * j * ATT_HEAD_DIM:
                POOL_WIDTH + 2 * (j + 1) * ATT_HEAD_DIM] = o2t.T.astype(BF16)
        if jj == ATT_HEADS // 4 - 1:
            rows_done(cp)


def _mix_kernel(x_ref, p_ref, halo_ref, q_ref, k_ref, vt_ref, biast_ref, wpool_ref, pscale_ref,
                wout_ref, o_ref, pbuf, mix, st_buf, *, tq):
    i = pl.program_id(1)

    def pool():
        halo = halo_ref[0]
        pbuf[0:POOL_HALO, :] = jnp.where(i == 0, jnp.zeros_like(halo), halo)
        pbuf[POOL_HALO:, :] = p_ref[0]
        t_abs = i * tq + lax.broadcasted_iota(jnp.int32, (tq, 1), 0)
        for gi, w in enumerate(POOL_WINDOWS):
            cols = slice(gi * POOL_GROUP_DIM, (gi + 1) * POOL_GROUP_DIM)
            cur = pbuf[POOL_HALO:, cols]
            acc = pbuf[:, cols]
            step = 1
            while step < w:
                acc = acc + pltpu.roll(acc, step, 0)
                step *= 2
            cnt = jnp.minimum(t_abs + 1, w).astype(F32)
            pooled = (acc[POOL_HALO:] / cnt - cur).astype(BF16)
            y = jnp.dot(pooled, wpool_ref[gi], preferred_element_type=F32) * pscale_ref[:, cols]
            mix[:, cols] = y.astype(BF16)

    def out_proj(cp):
        if cp % 2 == 1:
            rows = slice((cp - 1) * PAIR, (cp + 1) * PAIR)
            o_ref[0, rows, :] = x_ref[0, rows, :] + jnp.dot(mix[rows, :], wout_ref[...],
                                                          preferred_element_type=F32)

    def step(masked):
        pool()
        _attend_tile(q_ref, k_ref, vt_ref, biast_ref, mix, st_buf, i, tq, masked, out_proj)

    pl.when(i * tq < LEFT)(lambda: step(True))
    pl.when(i * tq >= LEFT)(lambda: step(False))


def _mix(x, p, q, k, vt, biast, w_pool, pool_scale, w_out):
    b, s, d = x.shape
    tq = 512
    halo_blocks = tq // POOL_HALO
    tile = lambda bi, i: (bi, i, 0)
    full = lambda bi, i: (bi, 0, 0)
    const2 = lambda bi, i: (0, 0)
    const3 = lambda bi, i: (0, 0, 0)
    return pl.pallas_call(
        functools.partial(_mix_kernel, tq=tq),
        grid=(b, s // tq),
        in_specs=[
            pl.BlockSpec((1, tq, d), tile),
            pl.BlockSpec((1, tq, POOL_WIDTH), tile),
            pl.BlockSpec((1, POOL_HALO, POOL_WIDTH),
                         lambda bi, i: (bi, jnp.maximum(i * halo_blocks - 1, 0), 0)),
            pl.BlockSpec((1, tq, ATT_WIDTH), tile),
            pl.BlockSpec((1, s + LEFT, ATT_WIDTH), full),
            pl.BlockSpec((1, VT_ROWS, s + LEFT), full),
            pl.BlockSpec(biast.shape, const3),
            pl.BlockSpec(w_pool.shape, const3),
            pl.BlockSpec(pool_scale.shape, const2),
            pl.BlockSpec(w_out.shape, const2),
        ],
        out_specs=pl.BlockSpec((1, tq, d), tile),
        out_shape=jax.ShapeDtypeStruct((b, s, d), F32),
        scratch_shapes=[
            pltpu.VMEM((POOL_HALO + tq, POOL_WIDTH), F32),
            pltpu.VMEM((tq, POOL_WIDTH + ATT_WIDTH), BF16),
            pltpu.VMEM((2, PAIR_BAND, 4 * PAIR), F32),
        ],
        compiler_params=pltpu.CompilerParams(
            dimension_semantics=("arbitrary", "arbitrary"), vmem_limit_bytes=VMEM_LIMIT),
        name="pool_attn_mix",
    )(x, p, p, q, k, vt, biast, w_pool, pool_scale, w_out)


def _sgu_kernel(h_ref, g_ref, win_ref, bin_ref, lng_ref, lnb_ref, ws_ref, bs_ref, wout_ref,
                o_ref, xn_ref, z_buf, gated, *, tm, sub):
    d = h_ref.shape[-1]
    hd = d // SGU_HEADS
    xn_ref[...] = _rms(h_ref[...], g_ref[...]).astype(BF16)
    row = lax.broadcasted_iota(jnp.int32, (SGU_CHUNK, SGU_CHUNK), 0)
    col = lax.broadcasted_iota(jnp.int32, (SGU_CHUNK, SGU_CHUNK), 1)
    ws = [jnp.where(row >= col, ws_ref[hh], jnp.zeros((), BF16)) for hh in range(SGU_HEADS)]

    def in_proj(s):
        z_buf[s % 2] = jnp.dot(xn_ref[s * sub:(s + 1) * sub, :], win_ref[...],
                               preferred_element_type=F32)

    in_proj(0)
    for s in range(tm // sub):
        if s + 1 < tm // sub:
            in_proj(s + 1)
        z = z_buf[s % 2] + bin_ref[...]
        z = 0.5 * z * (1.0 + lax.erf(z * (2.0 ** -0.5)))
        u = z[:, :d]
        v = z[:, d:]
        mu = jnp.mean(v, axis=-1, keepdims=True)
        vc = v - mu
        var = jnp.mean(vc * vc, axis=-1, keepdims=True)
        vn = (vc * lax.rsqrt(var + EPS) * lng_ref[...] + lnb_ref[...]).astype(BF16)
        nchunk = sub // SGU_CHUNK
        for hh in range(SGU_HEADS):
            cols = slice(hh * hd, (hh + 1) * hd)
            v_h = jnp.concatenate(
                [vn[n * SGU_CHUNK:(n + 1) * SGU_CHUNK, cols] for n in range(nchunk)], axis=1)
            mixed = jnp.dot(ws[hh], v_h, preferred_element_type=F32) + bs_ref[hh]
            for n in range(nchunk):
                rows = slice(n * SGU_CHUNK, (n + 1) * SGU_CHUNK)
                gated[s * sub + n * SGU_CHUNK:s * sub + (n + 1) * SGU_CHUNK, cols] = (
                    u[rows, cols] * mixed[:, n * hd:(n + 1) * hd]).astype(BF16)
        blk = slice(s * sub, (s + 1) * sub)
        o_ref[blk, :] = h_ref[blk, :] + jnp.dot(gated[blk, :], wout_ref[...],
                                                preferred_element_type=F32)


def _sgu(h, g, w_in, b_in, ln_g, ln_b, w_s, b_s, w_out):
    n, d = h.shape
    tm, sub = 1024, 256
    tile = lambda i: (i, 0)
    const2 = lambda i: (0, 0)
    const3 = lambda i: (0, 0, 0)
    return pl.pallas_call(
        functools.partial(_sgu_kernel, tm=tm, sub=sub),
        grid=(n // tm,),
        in_specs=[
            pl.BlockSpec((tm, d), tile),
            pl.BlockSpec((1, d), const2),
            pl.BlockSpec(w_in.shape, const2),
            pl.BlockSpec(b_in.shape, const2),
            pl.BlockSpec(ln_g.shape, const2),
            pl.BlockSpec(ln_b.shape, const2),
            pl.BlockSpec(w_s.shape, const3),
            pl.BlockSpec(b_s.shape, const3),
            pl.BlockSpec(w_out.shape, const2),
        ],
        out_specs=pl.BlockSpec((tm, d), tile),
        out_shape=jax.ShapeDtypeStruct((n, d), F32),
        scratch_shapes=[
            pltpu.VMEM((tm, d), BF16),
            pltpu.VMEM((2, sub, 2 * d), F32),
            pltpu.VMEM((tm, d), BF16),
        ],
        compiler_params=pltpu.CompilerParams(
            dimension_semantics=("arbitrary",), vmem_limit_bytes=VMEM_LIMIT),
        name="sgu",
    )(h, g, w_in, b_in, ln_g, ln_b, w_s, b_s, w_out)


def _route_rows(lt):
    neg = jnp.float32(-jnp.inf)

    def first_max(rows):
        m = functools.reduce(jnp.maximum, rows)
        idx = jnp.int32(len(rows))
        for r in reversed(range(len(rows))):
            idx = jnp.where(rows[r] == m, jnp.int32(r), idx)
        return m, idx

    gl = [lt[g:g + 1, :] for g in range(N_GROUPS)]
    gmax, gidx = first_max(gl)
    g_w = 1.0 / functools.reduce(jnp.add, [jnp.exp(x - gmax) for x in gl])
    el = []
    for j in range(EXPERTS_PER_GROUP):
        row = lt[N_GROUPS + j:N_GROUPS + j + 1, :]
        for g in range(1, N_GROUPS):
            r = N_GROUPS + EXPERTS_PER_GROUP * g + j
            row = jnp.where(gidx == g, lt[r:r + 1, :], row)
        el.append(row)
    m1, i1 = first_max(el)
    m2, i2 = first_max([jnp.where(i1 == j, neg, el[j]) for j in range(EXPERTS_PER_GROUP)])
    e2 = jnp.exp(m2 - m1)
    den = 1.0 + e2
    w1 = g_w / den
    w2 = g_w * e2 / den
    cw = [jnp.where(i1 == j, w1, 0.0) + jnp.where(i2 == j, w2, 0.0) for j in range(EXPERTS_PER_GROUP)]
    return gidx, cw


def _moe_kernel(h_ref, g_ref, wr_ref, br_ref, triu_ref, wg_ref, wu_ref, wd_ref, gf_ref,
                o_ref, xs_ref, ws_ref, ys_ref, pos_ref, hg_buf, hu_buf, act_ref, bounds, *, final_norm):
    step = pl.program_id(1)
    tm, d = h_ref.shape
    nperm = tm // PERM_BLOCK

    @pl.when(step == 0)
    def _():
        xn = _rms(h_ref[...], g_ref[...])
        x_hi = xn.astype(BF16)
        x_lo = (xn - x_hi.astype(F32)).astype(BF16)
        both = jnp.dot(x_hi, wr_ref[...], preferred_element_type=F32)
        logits = (both[:, :ROUTER_LANES] + both[:, ROUTER_LANES:]
                  + jnp.dot(x_lo, wr_ref[:, :ROUTER_LANES], preferred_element_type=F32)) + br_ref[...]
        gidx, cw = _route_rows(logits.T)

        row16 = lax.broadcasted_iota(jnp.int32, (ROW_ALIGN, tm), 0)
        onehot = jnp.where(row16 == gidx, 1.0, 0.0)
        rank = jnp.dot(onehot.astype(BF16), triu_ref[...], preferred_element_type=F32)
        pos_row = jnp.zeros((1, tm), F32)
        start = jnp.zeros((1, 1), F32)
        for g in range(N_GROUPS):
            bounds[g] = start[0, 0].astype(jnp.int32)
            mine = gidx == g
            pos_row = jnp.where(mine, rank[g:g + 1, :] + start, pos_row)
            start = start + jnp.sum(jnp.where(mine, 1.0, 0.0), axis=-1, keepdims=True)
        bounds[N_GROUPS] = jnp.int32(tm)

        packed = jnp.where(row16 == 0, pos_row, 0.0)
        for j in range(EXPERTS_PER_GROUP):
            piece = cw[j]
            for p in range(3):
                part = piece.astype(BF16).astype(F32)
                packed = packed + jnp.where(row16 == 1 + p * EXPERTS_PER_GROUP + j, part, 0.0)
                piece = piece - part
        packed = jnp.concatenate(
            [packed, jnp.zeros((ROUTER_LANES - ROW_ALIGN, tm), F32)], axis=0)
        tok = packed.T
        pos_ref[...] = tok
        w3 = tok.astype(BF16)
        pos_row = pos_row.astype(jnp.int32)

        for rb in range(nperm):
            rows = slice(rb * PERM_BLOCK, (rb + 1) * PERM_BLOCK)
            sorted_row = lax.broadcasted_iota(jnp.int32, (PERM_BLOCK, tm), 0) + rb * PERM_BLOCK
            perm = jnp.where(sorted_row == pos_row, 1.0, 0.0).astype(BF16)
            xs_ref[rows, :] = jnp.dot(perm, x_hi, preferred_element_type=F32).astype(BF16)
            ws_ref[rows, :] = jnp.dot(perm, w3, preferred_element_type=F32)
        ys_ref[...] = jnp.zeros_like(ys_ref)

    lane_b = lax.broadcasted_iota(jnp.int32, (MOE_BLOCK, ROUTER_LANES), 1)

    def expert_block(gg, seg_lo, seg_hi, blk_lo):
        row0 = pl.multiple_of(jnp.minimum(blk_lo, tm - MOE_BLOCK), ROW_ALIGN)
        rows = pl.ds(row0, MOE_BLOCK)
        sorted_row = lax.broadcasted_iota(jnp.int32, (MOE_BLOCK, 1), 0) + row0
        in_seg = ((sorted_row >= jnp.maximum(seg_lo, blk_lo))
                  & (sorted_row < jnp.minimum(seg_hi, blk_lo + MOE_BLOCK)))
        w3 = ws_ref[rows, :]

        def gate_up(j):
            e = gg * EXPERTS_PER_GROUP + j
            hg_buf[j % 2] = jnp.dot(xs_ref[rows, :], wg_ref[e], preferred_element_type=F32)
            hu_buf[j % 2] = jnp.dot(xs_ref[rows, :], wu_ref[e], preferred_element_type=F32)

        gate_up(0)
        for j in range(EXPERTS_PER_GROUP):
            if j + 1 < EXPERTS_PER_GROUP:
                gate_up(j + 1)
            mine = (((lane_b - 1) % EXPERTS_PER_GROUP == j) & (lane_b >= 1)
                    & (lane_b <= 3 * EXPERTS_PER_GROUP))
            c = jnp.sum(jnp.where(mine, w3, 0.0), axis=-1, keepdims=True)
            c = jnp.where(in_seg, c, 0.0)
            hg = hg_buf[j % 2]
            act = hg * (1.0 / (1.0 + jnp.exp(-hg))) * hu_buf[j % 2] * c
            act_ref[:, j * D_EXPERT:(j + 1) * D_EXPERT] = act.astype(BF16)
        ys_ref[rows, :] += jnp.dot(act_ref[...], wd_ref[gg], preferred_element_type=F32)

    for gg in range(MOE_GROUPS_PER_STEP):
        grp = step * MOE_GROUPS_PER_STEP + gg
        seg_lo = bounds[grp]
        seg_hi = bounds[grp + 1]
        first = (seg_lo // ROW_ALIGN) * ROW_ALIGN
        for k in range(pl.cdiv(tm, MOE_BLOCK)):
            blk_lo = first + k * MOE_BLOCK
            pl.when(blk_lo < seg_hi)(functools.partial(expert_block, gg, seg_lo, seg_hi, blk_lo))

    @pl.when(step == N_GROUPS // MOE_GROUPS_PER_STEP - 1)
    def _():
        xs_ref[...] = ys_ref[...].astype(BF16)
        for tb in range(nperm):
            rows = slice(tb * PERM_BLOCK, (tb + 1) * PERM_BLOCK)
            pos_i = pos_ref[rows, :1].astype(jnp.int32)
            perm_t = jnp.where(lax.broadcasted_iota(jnp.int32, (PERM_BLOCK, tm), 1) == pos_i,
                               1.0, 0.0).astype(BF16)
            out = h_ref[rows, :] + jnp.dot(perm_t, xs_ref[...], preferred_element_type=F32)
            if final_norm:
                out = _rms(out, gf_ref[...])
            o_ref[rows, :] = out


def _moe(h, g, wr, br, wg, wu, wd, gf, final_norm):
    n, d = h.shape
    tm = MOE_TILE
    tile = lambda i, e: (i, 0)
    const2 = lambda i, e: (0, 0)
    group = lambda i, e: (e, 0, 0)
    triu = jnp.tri(tm, tm, -1, dtype=BF16).T
    return pl.pallas_call(
        functools.partial(_moe_kernel, final_norm=final_norm),
        grid=(n // tm, N_GROUPS // MOE_GROUPS_PER_STEP),
        in_specs=[
            pl.BlockSpec((tm, d), tile),
            pl.BlockSpec((1, d), const2),
            pl.BlockSpec(wr.shape, const2),
            pl.BlockSpec(br.shape, const2),
            pl.BlockSpec(triu.shape, const2),
            pl.BlockSpec((MOE_GROUPS_PER_STEP * EXPERTS_PER_GROUP,) + wg.shape[1:], group),
            pl.BlockSpec((MOE_GROUPS_PER_STEP * EXPERTS_PER_GROUP,) + wu.shape[1:], group),
            pl.BlockSpec((MOE_GROUPS_PER_STEP,) + wd.shape[1:], group),
            pl.BlockSpec((1, d), const2),
        ],
        out_specs=pl.BlockSpec((tm, d), tile),
        out_shape=jax.ShapeDtypeStruct((n, d), F32),
        scratch_shapes=[
            pltpu.VMEM((tm, d), BF16),
            pltpu.VMEM((tm, ROUTER_LANES), F32),
            pltpu.VMEM((tm, d), F32),
            pltpu.VMEM((tm, ROUTER_LANES), F32),
            pltpu.VMEM((2, MOE_BLOCK, D_EXPERT), F32),
            pltpu.VMEM((2, MOE_BLOCK, D_EXPERT), F32),
            pltpu.VMEM((MOE_BLOCK, EXPERTS_PER_GROUP * D_EXPERT), BF16),
            pltpu.SMEM((8,), jnp.int32),
        ],
        compiler_params=pltpu.CompilerParams(
            dimension_semantics=("arbitrary", "arbitrary"), vmem_limit_bytes=VMEM_LIMIT),
        name="hier_moe",
    )(h, g, wr, br, triu, wg, wu, wd, gf)


def _router_params(wg_router, bg_router, we_router, be_router):
    d = wg_router.shape[0]
    we = jnp.transpose(we_router, (1, 0, 2)).reshape(d, N_EXPERTS)
    w = jnp.concatenate([wg_router, we], axis=1)
    w = jnp.pad(w, ((0, 0), (0, ROUTER_LANES - w.shape[1])))
    bias = jnp.concatenate([bg_router, be_router.reshape(N_EXPERTS)])
    bias = jnp.pad(bias, (0, ROUTER_LANES - bias.shape[0])).reshape(1, ROUTER_LANES)
    w_hi = w.astype(BF16)
    w_lo = (w - w_hi.astype(F32)).astype(BF16)
    return jnp.concatenate([w_hi, w_lo], axis=1), bias.astype(F32)


def _band_bias_t(rel_bias):
    width = PAIR_BAND + PAIR - 1
    m = jnp.arange(width + 1) - (PAIR - 1) - LEFT
    ext = rel_bias.astype(F32)[:, jnp.clip(m, -MAX_REL, MAX_REL) + MAX_REL]
    skew = jnp.tile(ext, (1, PAIR))[:, :PAIR * width].reshape(ATT_HEADS, PAIR, width)
    bias = skew[:, :, PAIR - 1:PAIR - 1 + PAIR_BAND]
    first = (jnp.arange(PAIR) // CHUNK * CHUNK)[:, None]
    kk = jnp.arange(PAIR_BAND)[None, :]
    bias = jnp.where(((kk >= first) & (kk < first + BAND))[None], bias * LOG2E, -1e30)
    bias_t = jnp.transpose(bias, (0, 2, 1)).reshape(ATT_HEADS // 2, 2, PAIR_BAND, PAIR)
    return jnp.transpose(bias_t, (0, 2, 1, 3)).reshape(ATT_HEADS // 2, PAIR_BAND, 2 * PAIR)


def kernel(x, norm_mix_g, norm_ffn_g, final_norm_g, ab_w_in, pool_w, pool_scale, att_rel_bias,
           ab_w_out, sgu_w_in, sgu_b_in, sgu_ln_g, sgu_ln_b, sgu_w_s, sgu_b_s, sgu_w_out,
           moe_wg_router, moe_bg_router, moe_we_router, moe_be_router,
           moe_w_gate, moe_w_up, moe_w_down):
    b, s, d = x.shape
    row = lambda a: a.reshape(1, -1).astype(F32)

    def moe(h, layer, final_norm):
        wr, br = _router_params(moe_wg_router[layer], moe_bg_router[layer],
                                moe_we_router[layer], moe_be_router[layer])
        wd = moe_w_down[layer].astype(BF16).reshape(N_GROUPS, EXPERTS_PER_GROUP * D_EXPERT, d)
        return _moe(h, row(norm_ffn_g[layer]), wr, br, moe_w_gate[layer].astype(BF16),
                    moe_w_up[layer].astype(BF16), wd, row(final_norm_g), final_norm)

    w_in = ab_w_in[0]
    n_pqk = POOL_WIDTH + 2 * ATT_WIDTH
    p, q, k, vt = _inproj(x, row(norm_mix_g[0]), w_in[:, :n_pqk].astype(BF16),
                          w_in[:, n_pqk:].T.astype(BF16))
    h = _mix(x, p, q, k, vt, _band_bias_t(att_rel_bias[0]), pool_w[0].astype(BF16),
             row(pool_scale[0]), ab_w_out[0].astype(BF16))
    h = moe(h.reshape(b * s, d), 0, False)

    h = _sgu(h, row(norm_mix_g[1]), sgu_w_in[0].astype(BF16), row(sgu_b_in[0]),
             row(sgu_ln_g[0]), row(sgu_ln_b[0]), sgu_w_s[0].astype(BF16),
             sgu_b_s[0].astype(F32)[:, :, None], sgu_w_out[0].astype(BF16))
    h = moe(h, 1, True)
    return h.reshape(b, s, d)
```

```python
import functools

import jax
import jax.numpy as jnp
from jax import lax
from jax.experimental import pallas as pl
from jax.experimental.pallas import tpu as pltpu

EPS = 1e-6
CHUNK = 64
POOL_WINDOWS = (2, 4, 8, 16)
POOL_GROUP_DIM = 128
POOL_WIDTH = 512
POOL_HALO = 16
ATT_HEADS = 8
ATT_HEAD_DIM = 64
ATT_WIDTH = 512
LEFT_CHUNKS = 8
LEFT = LEFT_CHUNKS * CHUNK
BAND = LEFT + CHUNK
PAIR = 2 * CHUNK
PAIR_BAND = LEFT + PAIR
MAX_REL = 128
ATT_SCALE = ATT_HEAD_DIM ** -0.5
LOG2E = 1.4426950408889634
VT_ONES = 16
VT_PAIR = 2 * ATT_HEAD_DIM + VT_ONES
VT_ROWS = (ATT_HEADS // 2) * VT_PAIR
SGU_CHUNK = 128
SGU_HEADS = 8
N_GROUPS = 4
EXPERTS_PER_GROUP = 4
N_EXPERTS = 16
D_EXPERT = 256
ROUTER_LANES = 128
MOE_TILE = 1024
MOE_BLOCK = 320
MOE_GROUPS_PER_STEP = 2
PERM_BLOCK = 256
ROW_ALIGN = 16

VMEM_LIMIT = 56 * 1024 * 1024

F32 = jnp.float32
BF16 = jnp.bfloat16


def _rms(x, g):
    return x * lax.rsqrt(jnp.mean(x * x, axis=-1, keepdims=True) + EPS) * g


def _inproj_kernel(x_ref, g_ref, w_ref, wvt_ref, p_ref, q_ref, k_ref, vt_ref):
    i = pl.program_id(1)

    @pl.when(i == 0)
    def _():
        k_ref[...] = jnp.zeros_like(k_ref)
        vt_ref[...] = jnp.zeros_like(vt_ref)

    @pl.when(i > 0)
    def _():
        xn = _rms(x_ref[0], g_ref[...]).astype(BF16)
        z = jnp.dot(xn, w_ref[...], preferred_element_type=F32)
        p_ref[0] = z[:, :POOL_WIDTH]
        q_ref[0] = (z[:, POOL_WIDTH:POOL_WIDTH + ATT_WIDTH] * (ATT_SCALE * LOG2E)).astype(BF16)
        k_ref[0] = z[:, POOL_WIDTH + ATT_WIDTH:].astype(BF16)
        vt = lax.dot_general(wvt_ref[...], xn, (((1,), (1,)), ((), ())), preferred_element_type=F32)
        for j in range(ATT_HEADS // 2):
            rows = slice(2 * j * ATT_HEAD_DIM, 2 * (j + 1) * ATT_HEAD_DIM)
            vt_ref[0, j * VT_PAIR:j * VT_PAIR + 2 * ATT_HEAD_DIM, :] = vt[rows].astype(BF16)
            vt_ref[0, j * VT_PAIR + 2 * ATT_HEAD_DIM:(j + 1) * VT_PAIR, :] = jnp.ones(
                (VT_ONES, vt.shape[1]), BF16)


def _inproj(x, g, w_pqk, w_vt):
    b, s, d = x.shape
    tm = LEFT
    nt = s // tm
    prev = lambda bi, i: (bi, jnp.maximum(i - 1, 0), 0)
    const2 = lambda bi, i: (0, 0)
    return pl.pallas_call(
        _inproj_kernel,
        grid=(b, nt + 1),
        in_specs=[
            pl.BlockSpec((1, tm, d), prev),
            pl.BlockSpec((1, d), const2),
            pl.BlockSpec(w_pqk.shape, const2),
            pl.BlockSpec(w_vt.shape, const2),
        ],
        out_specs=[
            pl.BlockSpec((1, tm, POOL_WIDTH), prev),
            pl.BlockSpec((1, tm, ATT_WIDTH), prev),
            pl.BlockSpec((1, tm, ATT_WIDTH), lambda bi, i: (bi, i, 0)),
            pl.BlockSpec((1, VT_ROWS, tm), lambda bi, i: (bi, 0, i)),
        ],
        out_shape=[
            jax.ShapeDtypeStruct((b, s, POOL_WIDTH), F32),
            jax.ShapeDtypeStruct((b, s, ATT_WIDTH), BF16),
            jax.ShapeDtypeStruct((b, s + LEFT, ATT_WIDTH), BF16),
            jax.ShapeDtypeStruct((b, VT_ROWS, s + LEFT), BF16),
        ],
        compiler_params=pltpu.CompilerParams(
            dimension_semantics=("arbitrary", "arbitrary"), vmem_limit_bytes=VMEM_LIMIT),
        name="inproj",
    )(x, g, w_pqk, w_vt)


def _attend_tile(q_ref, k_ref, vt_ref, biast_ref, mix, st_buf, i, tq, masked, rows_done):
    quad = 4 * ATT_HEAD_DIM
    lane = lax.broadcasted_iota(jnp.int32, (PAIR, quad), 1)
    zero = jnp.zeros((), BF16)
    units = [(cp, jj) for cp in range(tq // PAIR) for jj in range(ATT_HEADS // 4)]

    def band_base(cp):
        return pl.multiple_of((i * (tq // PAIR) + cp) * PAIR, PAIR)

    def scores(u):
        cp, jj = units[u]
        cols = slice(jj * quad, (jj + 1) * quad)
        q4 = q_ref[0, cp * PAIR:(cp + 1) * PAIR, cols]
        qs = jnp.concatenate(
            [jnp.where((lane >= h * ATT_HEAD_DIM) & (lane < (h + 1) * ATT_HEAD_DIM), q4, zero)
             for h in range(4)], axis=0)
        kb = k_ref[0, pl.ds(band_base(cp), PAIR_BAND), cols]
        st_buf[u % 2] = lax.dot_general(kb, qs, (((1,), (1,)), ((), ())),
                                        preferred_element_type=F32)

    scores(0)
    for u, (cp, jj) in enumerate(units):
        if u + 1 < len(units):
            scores(u + 1)
        base = band_base(cp)
        if masked:
            valid = (lax.broadcasted_iota(jnp.int32, (PAIR_BAND, 1), 0) + base - LEFT) >= 0
        for j in (2 * jj, 2 * jj + 1):
            st = st_buf[u % 2, :, (j % 2) * 2 * PAIR:(j % 2 + 1) * 2 * PAIR] + biast_ref[j]
            if masked:
                st = jnp.where(valid, st, -1e30)
            m = jnp.max(st, axis=0, keepdims=True)
            e = jnp.exp2(st - m).astype(BF16)
            vtb = vt_ref[0, j * VT_PAIR:(j + 1) * VT_PAIR, pl.ds(base, PAIR_BAND)]
            ot = jnp.dot(vtb, e, preferred_element_type=F32)
            ot = ot[:2 * ATT_HEAD_DIM] * (1.0 / ot[2 * ATT_HEAD_DIM:2 * ATT_HEAD_DIM + 1])
            o2t = jnp.concatenate([ot[:ATT_HEAD_DIM, :PAIR], ot[ATT_HEAD_DIM:, PAIR:]], axis=0)
            mix[cp * PAIR:(cp + 1) * PAIR, POOL_WIDTH + 2 * j * ATT_HEAD_DIM:
                POOL_WIDTH + 2 * (j + 1) * ATT_HEAD_DIM] = o2t.T.astype(BF16)
        if jj == ATT_HEADS // 4 - 1:
            rows_done(cp)


def _mix_kernel(x_ref, p_ref, halo_ref, q_ref, k_ref, vt_ref, biast_ref, wpool_ref, pscale_ref,
                wout_ref, o_ref, pbuf, mix, st_buf, *, tq):
    i = pl.program_id(1)

    def pool():
        halo = halo_ref[0]
        pbuf[0:POOL_HALO, :] = jnp.where(i == 0, jnp.zeros_like(halo), halo)
        pbuf[POOL_HALO:, :] = p_ref[0]
        t_abs = i * tq + lax.broadcasted_iota(jnp.int32, (tq, 1), 0)
        for gi, w in enumerate(POOL_WINDOWS):
            cols = slice(gi * POOL_GROUP_DIM, (gi + 1) * POOL_GROUP_DIM)
            cur = pbuf[POOL_HALO:, cols]
            acc = pbuf[:, cols]
            step = 1
            while step < w:
                acc = acc + pltpu.roll(acc, step, 0)
                step *= 2
            cnt = jnp.minimum(t_abs + 1, w).astype(F32)
            pooled = (acc[POOL_HALO:] / cnt - cur).astype(BF16)
            y = jnp.dot(pooled, wpool_ref[gi], preferred_element_type=F32) * pscale_ref[:, cols]
            mix[:, cols] = y.astype(BF16)

    def out_proj(cp):
        if cp % 2 == 1:
            rows = slice((cp - 1) * PAIR, (cp + 1) * PAIR)
            o_ref[0, rows, :] = x_ref[0, rows, :] + jnp.dot(mix[rows, :], wout_ref[...],
                                                          preferred_element_type=F32)

    def step(masked):
        pool()
        _attend_tile(q_ref, k_ref, vt_ref, biast_ref, mix, st_buf, i, tq, masked, out_proj)

    pl.when(i * tq < LEFT)(lambda: step(True))
    pl.when(i * tq >= LEFT)(lambda: step(False))


def _mix(x, p, q, k, vt, biast, w_pool, pool_scale, w_out):
    b, s, d = x.shape
    tq = 512
    halo_blocks = tq // POOL_HALO
    tile = lambda bi, i: (bi, i, 0)
    full = lambda bi, i: (bi, 0, 0)
    const2 = lambda bi, i: (0, 0)
    const3 = lambda bi, i: (0, 0, 0)
    return pl.pallas_call(
        functools.partial(_mix_kernel, tq=tq),
        grid=(b, s // tq),
        in_specs=[
            pl.BlockSpec((1, tq, d), tile),
            pl.BlockSpec((1, tq, POOL_WIDTH), tile),
            pl.BlockSpec((1, POOL_HALO, POOL_WIDTH),
                         lambda bi, i: (bi, jnp.maximum(i * halo_blocks - 1, 0), 0)),
            pl.BlockSpec((1, tq, ATT_WIDTH), tile),
            pl.BlockSpec((1, s + LEFT, ATT_WIDTH), full),
            pl.BlockSpec((1, VT_ROWS, s + LEFT), full),
            pl.BlockSpec(biast.shape, const3),
            pl.BlockSpec(w_pool.shape, const3),
            pl.BlockSpec(pool_scale.shape, const2),
            pl.BlockSpec(w_out.shape, const2),
        ],
        out_specs=pl.BlockSpec((1, tq, d), tile),
        out_shape=jax.ShapeDtypeStruct((b, s, d), F32),
        scratch_shapes=[
            pltpu.VMEM((POOL_HALO + tq, POOL_WIDTH), F32),
            pltpu.VMEM((tq, POOL_WIDTH + ATT_WIDTH), BF16),
            pltpu.VMEM((2, PAIR_BAND, 4 * PAIR), F32),
        ],
        compiler_params=pltpu.CompilerParams(
            dimension_semantics=("arbitrary", "arbitrary"), vmem_limit_bytes=VMEM_LIMIT),
        name="pool_attn_mix",
    )(x, p, p, q, k, vt, biast, w_pool, pool_scale, w_out)


def _sgu_kernel(h_ref, g_ref, win_ref, bin_ref, lng_ref, lnb_ref, ws_ref, bs_ref, wout_ref,
                o_ref, xn_ref, z_buf, gated, *, tm, sub):
    d = h_ref.shape[-1]
    hd = d // SGU_HEADS
    xn_ref[...] = _rms(h_ref[...], g_ref[...]).astype(BF16)
    row = lax.broadcasted_iota(jnp.int32, (SGU_CHUNK, SGU_CHUNK), 0)
    col = lax.broadcasted_iota(jnp.int32, (SGU_CHUNK, SGU_CHUNK), 1)
    ws = [jnp.where(row >= col, ws_ref[hh], jnp.zeros((), BF16)) for hh in range(SGU_HEADS)]

    def in_proj(s):
        z_buf[s % 2] = jnp.dot(xn_ref[s * sub:(s + 1) * sub, :], win_ref[...],
                               preferred_element_type=F32)

    in_proj(0)
    for s in range(tm // sub):
        if s + 1 < tm // sub:
            in_proj(s + 1)
        z = z_buf[s % 2] + bin_ref[...]
        z = 0.5 * z * (1.0 + lax.erf(z * (2.0 ** -0.5)))
        u = z[:, :d]
        v = z[:, d:]
        mu = jnp.mean(v, axis=-1, keepdims=True)
        vc = v - mu
        var = jnp.mean(vc * vc, axis=-1, keepdims=True)
        vn = (vc * lax.rsqrt(var + EPS) * lng_ref[...] + lnb_ref[...]).astype(BF16)
        nchunk = sub // SGU_CHUNK
        for hh in range(SGU_HEADS):
            cols = slice(hh * hd, (hh + 1) * hd)
            v_h = jnp.concatenate(
                [vn[n * SGU_CHUNK:(n + 1) * SGU_CHUNK, cols] for n in range(nchunk)], axis=1)
            mixed = jnp.dot(ws[hh], v_h, preferred_element_type=F32) + bs_ref[hh]
            for n in range(nchunk):
                rows = slice(n * SGU_CHUNK, (n + 1) * SGU_CHUNK)
                gated[s * sub + n * SGU_CHUNK:s * sub + (n + 1) * SGU_CHUNK, cols] = (
                    u[rows, cols] * mixed[:, n * hd:(n + 1) * hd]).astype(BF16)
        blk = slice(s * sub, (s + 1) * sub)
        o_ref[blk, :] = h_ref[blk, :] + jnp.dot(gated[blk, :], wout_ref[...],
                                                preferred_element_type=F32)


def _sgu(h, g, w_in, b_in, ln_g, ln_b, w_s, b_s, w_out):
    n, d = h.shape
    tm, sub = 1024, 256
    tile = lambda i: (i, 0)
    const2 = lambda i: (0, 0)
    const3 = lambda i: (0, 0, 0)
    return pl.pallas_call(
        functools.partial(_sgu_kernel, tm=tm, sub=sub),
        grid=(n // tm,),
        in_specs=[
            pl.BlockSpec((tm, d), tile),
            pl.BlockSpec((1, d), const2),
            pl.BlockSpec(w_in.shape, const2),
            pl.BlockSpec(b_in.shape, const2),
            pl.BlockSpec(ln_g.shape, const2),
            pl.BlockSpec(ln_b.shape, const2),
            pl.BlockSpec(w_s.shape, const3),
            pl.BlockSpec(b_s.shape, const3),
            pl.BlockSpec(w_out.shape, const2),
        ],
        out_specs=pl.BlockSpec((tm, d), tile),
        out_shape=jax.ShapeDtypeStruct((n, d), F32),
        scratch_shapes=[
            pltpu.VMEM((tm, d), BF16),
            pltpu.VMEM((2, sub, 2 * d), F32),
            pltpu.VMEM((tm, d), BF16),
        ],
        compiler_params=pltpu.CompilerParams(
            dimension_semantics=("arbitrary",), vmem_limit_bytes=VMEM_LIMIT),
        name="sgu",
    )(h, g, w_in, b_in, ln_g, ln_b, w_s, b_s, w_out)


def _route_rows(lt):
    neg = jnp.float32(-jnp.inf)

    def first_max(rows):
        m = functools.reduce(jnp.maximum, rows)
        idx = jnp.int32(len(rows))
        for r in reversed(range(len(rows))):
            idx = jnp.where(rows[r] == m, jnp.int32(r), idx)
        return m, idx

    gl = [lt[g:g + 1, :] for g in range(N_GROUPS)]
    gmax, gidx = first_max(gl)
    g_w = 1.0 / functools.reduce(jnp.add, [jnp.exp(x - gmax) for x in gl])
    el = []
    for j in range(EXPERTS_PER_GROUP):
        row = lt[N_GROUPS + j:N_GROUPS + j + 1, :]
        for g in range(1, N_GROUPS):
            r = N_GROUPS + EXPERTS_PER_GROUP * g + j
            row = jnp.where(gidx == g, lt[r:r + 1, :], row)
        el.append(row)
    m1, i1 = first_max(el)
    m2, i2 = first_max([jnp.where(i1 == j, neg, el[j]) for j in range(EXPERTS_PER_GROUP)])
    e2 = jnp.exp(m2 - m1)
    den = 1.0 + e2
    w1 = g_w / den
    w2 = g_w * e2 / den
    cw = [jnp.where(i1 == j, w1, 0.0) + jnp.where(i2 == j, w2, 0.0) for j in range(EXPERTS_PER_GROUP)]
    return gidx, cw


def _moe_kernel(h_ref, g_ref, wr_ref, br_ref, triu_ref, wg_ref, wu_ref, wd_ref, gf_ref,
                o_ref, xs_ref, ws_ref, ys_ref, pos_ref, hg_buf, hu_buf, act_ref, bounds, *, final_norm):
    step = pl.program_id(1)
    tm, d = h_ref.shape
    nperm = tm // PERM_BLOCK

    @pl.when(step == 0)
    def _():
        xn = _rms(h_ref[...], g_ref[...])
        x_hi = xn.astype(BF16)
        x_lo = (xn - x_hi.astype(F32)).astype(BF16)
        both = jnp.dot(x_hi, wr_ref[...], preferred_element_type=F32)
        logits = (both[:, :ROUTER_LANES] + both[:, ROUTER_LANES:]
                  + jnp.dot(x_lo, wr_ref[:, :ROUTER_LANES], preferred_element_type=F32)) + br_ref[...]
        gidx, cw = _route_rows(logits.T)

        row16 = lax.broadcasted_iota(jnp.int32, (ROW_ALIGN, tm), 0)
        onehot = jnp.where(row16 == gidx, 1.0, 0.0)
        rank = jnp.dot(onehot.astype(BF16), triu_ref[...], preferred_element_type=F32)
        pos_row = jnp.zeros((1, tm), F32)
        start = jnp.zeros((1, 1), F32)
        for g in range(N_GROUPS):
            bounds[g] = start[0, 0].astype(jnp.int32)
            mine = gidx == g
            pos_row = jnp.where(mine, rank[g:g + 1, :] + start, pos_row)
            start = start + jnp.sum(jnp.where(mine, 1.0, 0.0), axis=-1, keepdims=True)
        bounds[N_GROUPS] = jnp.int32(tm)

        packed = jnp.where(row16 == 0, pos_row, 0.0)
        for j in range(EXPERTS_PER_GROUP):
            piece = cw[j]
            for p in range(3):
                part = piece.astype(BF16).astype(F32)
                packed = packed + jnp.where(row16 == 1 + p * EXPERTS_PER_GROUP + j, part, 0.0)
                piece = piece - part
        packed = jnp.concatenate(
            [packed, jnp.zeros((ROUTER_LANES - ROW_ALIGN, tm), F32)], axis=0)
        tok = packed.T
        pos_ref[...] = tok
        w3 = tok.astype(BF16)
        pos_row = pos_row.astype(jnp.int32)

        for rb in range(nperm):
            rows = slice(rb * PERM_BLOCK, (rb + 1) * PERM_BLOCK)
            sorted_row = lax.broadcasted_iota(jnp.int32, (PERM_BLOCK, tm), 0) + rb * PERM_BLOCK
            perm = jnp.where(sorted_row == pos_row, 1.0, 0.0).astype(BF16)
            xs_ref[rows, :] = jnp.dot(perm, x_hi, preferred_element_type=F32).astype(BF16)
            ws_ref[rows, :] = jnp.dot(perm, w3, preferred_element_type=F32)
        ys_ref[...] = jnp.zeros_like(ys_ref)

    lane_b = lax.broadcasted_iota(jnp.int32, (MOE_BLOCK, ROUTER_LANES), 1)

    def expert_block(gg, seg_lo, seg_hi, blk_lo):
        row0 = pl.multiple_of(jnp.minimum(blk_lo, tm - MOE_BLOCK), ROW_ALIGN)
        rows = pl.ds(row0, MOE_BLOCK)
        sorted_row = lax.broadcasted_iota(jnp.int32, (MOE_BLOCK, 1), 0) + row0
        in_seg = ((sorted_row >= jnp.maximum(seg_lo, blk_lo))
                  & (sorted_row < jnp.minimum(seg_hi, blk_lo + MOE_BLOCK)))
        w3 = ws_ref[rows, :]

        def gate_up(j):
            e = gg * EXPERTS_PER_GROUP + j
            hg_buf[j % 2] = jnp.dot(xs_ref[rows, :], wg_ref[0, e], preferred_element_type=F32)
            hu_buf[j % 2] = jnp.dot(xs_ref[rows, :], wu_ref[0, e], preferred_element_type=F32)

        gate_up(0)
        for j in range(EXPERTS_PER_GROUP):
            if j + 1 < EXPERTS_PER_GROUP:
                gate_up(j + 1)
            mine = (((lane_b - 1) % EXPERTS_PER_GROUP == j) & (lane_b >= 1)
                    & (lane_b <= 3 * EXPERTS_PER_GROUP))
            c = jnp.sum(jnp.where(mine, w3, 0.0), axis=-1, keepdims=True)
            c = jnp.where(in_seg, c, 0.0)
            hg = hg_buf[j % 2]
            act = hg * (1.0 / (1.0 + jnp.exp(-hg))) * hu_buf[j % 2] * c
            act_ref[:, j * D_EXPERT:(j + 1) * D_EXPERT] = act.astype(BF16)
        ys_ref[rows, :] += jnp.dot(act_ref[...], wd_ref[0, gg], preferred_element_type=F32)

    for gg in range(MOE_GROUPS_PER_STEP):
        grp = step * MOE_GROUPS_PER_STEP + gg
        seg_lo = bounds[grp]
        seg_hi = bounds[grp + 1]
        first = (seg_lo // ROW_ALIGN) * ROW_ALIGN
        for k in range(pl.cdiv(tm, MOE_BLOCK)):
            blk_lo = first + k * MOE_BLOCK
            pl.when(blk_lo < seg_hi)(functools.partial(expert_block, gg, seg_lo, seg_hi, blk_lo))

    @pl.when(step == N_GROUPS // MOE_GROUPS_PER_STEP - 1)
    def _():
        xs_ref[...] = ys_ref[...].astype(BF16)
        for tb in range(nperm):
            rows = slice(tb * PERM_BLOCK, (tb + 1) * PERM_BLOCK)
            pos_i = pos_ref[rows, :1].astype(jnp.int32)
            perm_t = jnp.where(lax.broadcasted_iota(jnp.int32, (PERM_BLOCK, tm), 1) == pos_i,
                               1.0, 0.0).astype(BF16)
            out = h_ref[rows, :] + jnp.dot(perm_t, xs_ref[...], preferred_element_type=F32)
            if final_norm:
                out = _rms(out, gf_ref[...])
            o_ref[rows, :] = out


def _moe(h, g, wr, br, wg, wu, wd, gf, layer, final_norm):
    n, d = h.shape
    tm = MOE_TILE
    tile = lambda i, e: (i, 0)
    const2 = lambda i, e: (0, 0)
    group = lambda i, e: (layer, e, 0, 0)
    triu = jnp.tri(tm, tm, -1, dtype=BF16).T
    return pl.pallas_call(
        functools.partial(_moe_kernel, final_norm=final_norm),
        grid=(n // tm, N_GROUPS // MOE_GROUPS_PER_STEP),
        in_specs=[
            pl.BlockSpec((tm, d), tile),
            pl.BlockSpec((1, d), const2),
            pl.BlockSpec(wr.shape, const2),
            pl.BlockSpec(br.shape, const2),
            pl.BlockSpec(triu.shape, const2),
            pl.BlockSpec((1, MOE_GROUPS_PER_STEP * EXPERTS_PER_GROUP) + wg.shape[2:], group),
            pl.BlockSpec((1, MOE_GROUPS_PER_STEP * EXPERTS_PER_GROUP) + wu.shape[2:], group),
            pl.BlockSpec((1, MOE_GROUPS_PER_STEP) + wd.shape[2:], group),
            pl.BlockSpec((1, d), const2),
        ],
        out_specs=pl.BlockSpec((tm, d), tile),
        out_shape=jax.ShapeDtypeStruct((n, d), F32),
        scratch_shapes=[
            pltpu.VMEM((tm, d), BF16),
            pltpu.VMEM((tm, ROUTER_LANES), F32),
            pltpu.VMEM((tm, d), F32),
            pltpu.VMEM((tm, ROUTER_LANES), F32),
            pltpu.VMEM((2, MOE_BLOCK, D_EXPERT), F32),
            pltpu.VMEM((2, MOE_BLOCK, D_EXPERT), F32),
            pltpu.VMEM((MOE_BLOCK, EXPERTS_PER_GROUP * D_EXPERT), BF16),
            pltpu.SMEM((8,), jnp.int32),
        ],
        compiler_params=pltpu.CompilerParams(
            dimension_semantics=("arbitrary", "arbitrary"), vmem_limit_bytes=VMEM_LIMIT),
        name="hier_moe",
    )(h, g, wr, br, triu, wg, wu, wd, gf)


def _cast_kernel(x_ref, o_ref):
    o_ref[...] = x_ref[...].astype(o_ref.dtype)


def _to_bf16(w):
    l, e, a, b = w.shape
    blk = (1, EXPERTS_PER_GROUP, a, b)
    spec = pl.BlockSpec(blk, lambda i, j: (i, j, 0, 0))
    return pl.pallas_call(
        _cast_kernel,
        grid=(l, e // EXPERTS_PER_GROUP),
        in_specs=[spec],
        out_specs=spec,
        out_shape=jax.ShapeDtypeStruct(w.shape, BF16),
        compiler_params=pltpu.CompilerParams(
            dimension_semantics=("arbitrary", "arbitrary"), vmem_limit_bytes=VMEM_LIMIT),
        name="cast_bf16",
    )(w)


def _router_params(wg_router, bg_router, we_router, be_router):
    d = wg_router.shape[0]
    we = jnp.transpose(we_router, (1, 0, 2)).reshape(d, N_EXPERTS)
    w = jnp.concatenate([wg_router, we], axis=1)
    w = jnp.pad(w, ((0, 0), (0, ROUTER_LANES - w.shape[1])))
    bias = jnp.concatenate([bg_router, be_router.reshape(N_EXPERTS)])
    bias = jnp.pad(bias, (0, ROUTER_LANES - bias.shape[0])).reshape(1, ROUTER_LANES)
    w_hi = w.astype(BF16)
    w_lo = (w - w_hi.astype(F32)).astype(BF16)
    return jnp.concatenate([w_hi, w_lo], axis=1), bias.astype(F32)


def _band_bias_t(rel_bias):
    width = PAIR_BAND + PAIR - 1
    m = jnp.arange(width + 1) - (PAIR - 1) - LEFT
    ext = rel_bias.astype(F32)[:, jnp.clip(m, -MAX_REL, MAX_REL) + MAX_REL]
    skew = jnp.tile(ext, (1, PAIR))[:, :PAIR * width].reshape(ATT_HEADS, PAIR, width)
    bias = skew[:, :, PAIR - 1:PAIR - 1 + PAIR_BAND]
    first = (jnp.arange(PAIR) // CHUNK * CHUNK)[:, None]
    kk = jnp.arange(PAIR_BAND)[None, :]
    bias = jnp.where(((kk >= first) & (kk < first + BAND))[None], bias * LOG2E, -1e30)
    bias_t = jnp.transpose(bias, (0, 2, 1)).reshape(ATT_HEADS // 2, 2, PAIR_BAND, PAIR)
    return jnp.transpose(bias_t, (0, 2, 1, 3)).reshape(ATT_HEADS // 2, PAIR_BAND, 2 * PAIR)


def kernel(x, norm_mix_g, norm_ffn_g, final_norm_g, ab_w_in, pool_w, pool_scale, att_rel_bias,
           ab_w_out, sgu_w_in, sgu_b_in, sgu_ln_g, sgu_ln_b, sgu_w_s, sgu_b_s, sgu_w_out,
           moe_wg_router, moe_bg_router, moe_we_router, moe_be_router,
           moe_w_gate, moe_w_up, moe_w_down):
    b, s, d = x.shape
    row = lambda a: a.reshape(1, -1).astype(F32)

    wg_all = _to_bf16(moe_w_gate)
    wu_all = _to_bf16(moe_w_up)
    wd_all = _to_bf16(moe_w_down).reshape(-1, N_GROUPS, EXPERTS_PER_GROUP * D_EXPERT, d)

    def moe(h, layer, final_norm):
        wr, br = _router_params(moe_wg_router[layer], moe_bg_router[layer],
                                moe_we_router[layer], moe_be_router[layer])
        return _moe(h, row(norm_ffn_g[layer]), wr, br, wg_all, wu_all, wd_all, row(final_norm_g),
                    layer, final_norm)

    w_in = ab_w_in[0]
    n_pqk = POOL_WIDTH + 2 * ATT_WIDTH
    p, q, k, vt = _inproj(x, row(norm_mix_g[0]), w_in[:, :n_pqk].astype(BF16),
                          w_in[:, n_pqk:].T.astype(BF16))
    h = _mix(x, p, q, k, vt, _band_bias_t(att_rel_bias[0]), pool_w[0].astype(BF16),
             row(pool_scale[0]), ab_w_out[0].astype(BF16))
    h = moe(h.reshape(b * s, d), 0, False)

    h = _sgu(h, row(norm_mix_g[1]), sgu_w_in[0].astype(BF16), row(sgu_b_in[0]),
             row(sgu_ln_g[0]), row(sgu_ln_b[0]), sgu_w_s[0].astype(BF16),
             sgu_b_s[0].astype(F32)[:, :, None], sgu_w_out[0].astype(BF16))
    h = moe(h, 1, True)
    return h.reshape(b, s, d)
```

```python
import functools

import jax
import jax.numpy as jnp
from jax import lax
from jax.experimental import pallas as pl
from jax.experimental.pallas import tpu as pltpu

EPS = 1e-6
CHUNK = 64
POOL_WINDOWS = (2, 4, 8, 16)
POOL_GROUP_DIM = 128
POOL_WIDTH = 512
POOL_HALO = 16
ATT_HEADS = 8
ATT_HEAD_DIM = 64
ATT_WIDTH = 512
LEFT_CHUNKS = 8
LEFT = LEFT_CHUNKS * CHUNK
BAND = LEFT + CHUNK
PAIR = 2 * CHUNK
PAIR_BAND = LEFT + PAIR
MAX_REL = 128
ATT_SCALE = ATT_HEAD_DIM ** -0.5
LOG2E = 1.4426950408889634
VT_ONES = 16
VT_PAIR = 2 * ATT_HEAD_DIM + VT_ONES
VT_ROWS = (ATT_HEADS // 2) * VT_PAIR
SGU_CHUNK = 128
SGU_HEADS = 8
N_GROUPS = 4
EXPERTS_PER_GROUP = 4
N_EXPERTS = 16
D_EXPERT = 256
ROUTER_LANES = 128
MOE_TILE = 1024
MOE_BLOCK = 320
MOE_GROUPS_PER_STEP = 2
PERM_BLOCK = 256
ROW_ALIGN = 16

VMEM_LIMIT = 56 * 1024 * 1024

F32 = jnp.float32
BF16 = jnp.bfloat16


def _rms(x, g):
    return x * lax.rsqrt(jnp.mean(x * x, axis=-1, keepdims=True) + EPS) * g


def _inproj_kernel(x_ref, g_ref, w_ref, wvt_ref, p_ref, q_ref, k_ref, vt_ref):
    i = pl.program_id(1)

    @pl.when(i == 0)
    def _():
        k_ref[...] = jnp.zeros_like(k_ref)
        vt_ref[...] = jnp.zeros_like(vt_ref)

    @pl.when(i > 0)
    def _():
        xn = _rms(x_ref[0], g_ref[...]).astype(BF16)
        z = jnp.dot(xn, w_ref[...], preferred_element_type=F32)
        p_ref[0] = z[:, :POOL_WIDTH]
        q_ref[0] = (z[:, POOL_WIDTH:POOL_WIDTH + ATT_WIDTH] * (ATT_SCALE * LOG2E)).astype(BF16)
        k_ref[0] = z[:, POOL_WIDTH + ATT_WIDTH:].astype(BF16)
        vt = lax.dot_general(wvt_ref[...], xn, (((1,), (1,)), ((), ())), preferred_element_type=F32)
        for j in range(ATT_HEADS // 2):
            rows = slice(2 * j * ATT_HEAD_DIM, 2 * (j + 1) * ATT_HEAD_DIM)
            vt_ref[0, j * VT_PAIR:j * VT_PAIR + 2 * ATT_HEAD_DIM, :] = vt[rows].astype(BF16)
            vt_ref[0, j * VT_PAIR + 2 * ATT_HEAD_DIM:(j + 1) * VT_PAIR, :] = jnp.ones(
                (VT_ONES, vt.shape[1]), BF16)


def _inproj(x, g, w_pqk, w_vt):
    b, s, d = x.shape
    tm = LEFT
    nt = s // tm
    prev = lambda bi, i: (bi, jnp.maximum(i - 1, 0), 0)
    const2 = lambda bi, i: (0, 0)
    return pl.pallas_call(
        _inproj_kernel,
        grid=(b, nt + 1),
        in_specs=[
            pl.BlockSpec((1, tm, d), prev),
            pl.BlockSpec((1, d), const2),
            pl.BlockSpec(w_pqk.shape, const2),
            pl.BlockSpec(w_vt.shape, const2),
        ],
        out_specs=[
            pl.BlockSpec((1, tm, POOL_WIDTH), prev),
            pl.BlockSpec((1, tm, ATT_WIDTH), prev),
            pl.BlockSpec((1, tm, ATT_WIDTH), lambda bi, i: (bi, i, 0)),
            pl.BlockSpec((1, VT_ROWS, tm), lambda bi, i: (bi, 0, i)),
        ],
        out_shape=[
            jax.ShapeDtypeStruct((b, s, POOL_WIDTH), F32),
            jax.ShapeDtypeStruct((b, s, ATT_WIDTH), BF16),
            jax.ShapeDtypeStruct((b, s + LEFT, ATT_WIDTH), BF16),
            jax.ShapeDtypeStruct((b, VT_ROWS, s + LEFT), BF16),
        ],
        compiler_params=pltpu.CompilerParams(
            dimension_semantics=("arbitrary", "arbitrary"), vmem_limit_bytes=VMEM_LIMIT),
        name="inproj",
    )(x, g, w_pqk, w_vt)


def _attend_tile(q_ref, k_ref, vt_ref, biast_ref, mix, st_buf, i, tq, masked, side_work, rows_done):
    quad = 4 * ATT_HEAD_DIM
    lane = lax.broadcasted_iota(jnp.int32, (PAIR, quad), 1)
    zero = jnp.zeros((), BF16)
    units = [(cp, jj) for cp in range(tq // PAIR) for jj in range(ATT_HEADS // 4)]

    def band_base(cp):
        return pl.multiple_of((i * (tq // PAIR) + cp) * PAIR, PAIR)

    def scores(u):
        cp, jj = units[u]
        cols = slice(jj * quad, (jj + 1) * quad)
        q4 = q_ref[0, cp * PAIR:(cp + 1) * PAIR, cols]
        qs = jnp.concatenate(
            [jnp.where((lane >= h * ATT_HEAD_DIM) & (lane < (h + 1) * ATT_HEAD_DIM), q4, zero)
             for h in range(4)], axis=0)
        kb = k_ref[0, pl.ds(band_base(cp), PAIR_BAND), cols]
        st_buf[u % 2] = lax.dot_general(kb, qs, (((1,), (1,)), ((), ())),
                                        preferred_element_type=F32)

    scores(0)
    for u, (cp, jj) in enumerate(units):
        if u + 1 < len(units):
            scores(u + 1)
        if u < len(side_work):
            side_work[u]()
        base = band_base(cp)
        if masked:
            valid = (lax.broadcasted_iota(jnp.int32, (PAIR_BAND, 1), 0) + base - LEFT) >= 0
        for j in (2 * jj, 2 * jj + 1):
            st = st_buf[u % 2, :, (j % 2) * 2 * PAIR:(j % 2 + 1) * 2 * PAIR] + biast_ref[j]
            if masked:
                st = jnp.where(valid, st, -1e30)
            m = jnp.max(st, axis=0, keepdims=True)
            e = jnp.exp2(st - m).astype(BF16)
            vtb = vt_ref[0, j * VT_PAIR:(j + 1) * VT_PAIR, pl.ds(base, PAIR_BAND)]
            ot = jnp.dot(vtb, e, preferred_element_type=F32)
            ot = ot[:2 * ATT_HEAD_DIM] * (1.0 / ot[2 * ATT_HEAD_DIM:2 * ATT_HEAD_DIM + 1])
            o2t = jnp.concatenate([ot[:ATT_HEAD_DIM, :PAIR], ot[ATT_HEAD_DIM:, PAIR:]], axis=0)
            mix[cp * PAIR:(cp + 1) * PAIR, POOL_WIDTH + 2 * j * ATT_HEAD_DIM:
                POOL_WIDTH + 2 * (j + 1) * ATT_HEAD_DIM] = o2t.T.astype(BF16)
        if jj == ATT_HEADS // 4 - 1:
            rows_done(cp)


def _mix_kernel(x_ref, p_ref, halo_ref, q_ref, k_ref, vt_ref, biast_ref, wpool_ref, pscale_ref,
                wout_ref, o_ref, pbuf, mix, st_buf, *, tq):
    i = pl.program_id(1)

    def pool_fill():
        halo = halo_ref[0]
        pbuf[0:POOL_HALO, :] = jnp.where(i == 0, jnp.zeros_like(halo), halo)
        pbuf[POOL_HALO:, :] = p_ref[0]

    def pool(gi):
        w = POOL_WINDOWS[gi]
        cols = slice(gi * POOL_GROUP_DIM, (gi + 1) * POOL_GROUP_DIM)
        cur = pbuf[POOL_HALO:, cols]
        acc = pbuf[:, cols]
        step = 1
        while step < w:
            acc = acc + pltpu.roll(acc, step, 0)
            step *= 2
        t_abs = i * tq + lax.broadcasted_iota(jnp.int32, (tq, 1), 0)
        cnt = jnp.minimum(t_abs + 1, w).astype(F32)
        pooled = (acc[POOL_HALO:] / cnt - cur).astype(BF16)
        y = jnp.dot(pooled, wpool_ref[gi], preferred_element_type=F32) * pscale_ref[:, cols]
        mix[:, cols] = y.astype(BF16)

    def out_proj(cp):
        if cp % 2 == 1:
            rows = slice((cp - 1) * PAIR, (cp + 1) * PAIR)
            o_ref[0, rows, :] = x_ref[0, rows, :] + jnp.dot(mix[rows, :], wout_ref[...],
                                                          preferred_element_type=F32)

    def step(masked):
        pool_fill()
        side_work = [functools.partial(pool, gi) for gi in range(len(POOL_WINDOWS))]
        _attend_tile(q_ref, k_ref, vt_ref, biast_ref, mix, st_buf, i, tq, masked, side_work, out_proj)

    pl.when(i * tq < LEFT)(lambda: step(True))
    pl.when(i * tq >= LEFT)(lambda: step(False))


def _mix(x, p, q, k, vt, biast, w_pool, pool_scale, w_out):
    b, s, d = x.shape
    tq = 512
    halo_blocks = tq // POOL_HALO
    tile = lambda bi, i: (bi, i, 0)
    full = lambda bi, i: (bi, 0, 0)
    const2 = lambda bi, i: (0, 0)
    const3 = lambda bi, i: (0, 0, 0)
    return pl.pallas_call(
        functools.partial(_mix_kernel, tq=tq),
        grid=(b, s // tq),
        in_specs=[
            pl.BlockSpec((1, tq, d), tile),
            pl.BlockSpec((1, tq, POOL_WIDTH), tile),
            pl.BlockSpec((1, POOL_HALO, POOL_WIDTH),
                         lambda bi, i: (bi, jnp.maximum(i * halo_blocks - 1, 0), 0)),
            pl.BlockSpec((1, tq, ATT_WIDTH), tile),
            pl.BlockSpec((1, s + LEFT, ATT_WIDTH), full),
            pl.BlockSpec((1, VT_ROWS, s + LEFT), full),
            pl.BlockSpec(biast.shape, const3),
            pl.BlockSpec(w_pool.shape, const3),
            pl.BlockSpec(pool_scale.shape, const2),
            pl.BlockSpec(w_out.shape, const2),
        ],
        out_specs=pl.BlockSpec((1, tq, d), tile),
        out_shape=jax.ShapeDtypeStruct((b, s, d), F32),
        scratch_shapes=[
            pltpu.VMEM((POOL_HALO + tq, POOL_WIDTH), F32),
            pltpu.VMEM((tq, POOL_WIDTH + ATT_WIDTH), BF16),
            pltpu.VMEM((2, PAIR_BAND, 4 * PAIR), F32),
        ],
        compiler_params=pltpu.CompilerParams(
            dimension_semantics=("arbitrary", "arbitrary"), vmem_limit_bytes=VMEM_LIMIT),
        name="pool_attn_mix",
    )(x, p, p, q, k, vt, biast, w_pool, pool_scale, w_out)


def _sgu_kernel(h_ref, g_ref, win_ref, bin_ref, lng_ref, lnb_ref, ws_ref, bs_ref, wout_ref,
                o_ref, xn_ref, z_buf, gated, *, tm, sub):
    d = h_ref.shape[-1]
    hd = d // SGU_HEADS
    xn_ref[...] = _rms(h_ref[...], g_ref[...]).astype(BF16)
    row = lax.broadcasted_iota(jnp.int32, (SGU_CHUNK, SGU_CHUNK), 0)
    col = lax.broadcasted_iota(jnp.int32, (SGU_CHUNK, SGU_CHUNK), 1)
    ws = [jnp.where(row >= col, ws_ref[hh], jnp.zeros((), BF16)) for hh in range(SGU_HEADS)]

    def in_proj(s):
        z_buf[s % 2] = jnp.dot(xn_ref[s * sub:(s + 1) * sub, :], win_ref[...],
                               preferred_element_type=F32)

    in_proj(0)
    for s in range(tm // sub):
        if s + 1 < tm // sub:
            in_proj(s + 1)
        z = z_buf[s % 2] + bin_ref[...]
        z = 0.5 * z * (1.0 + lax.erf(z * (2.0 ** -0.5)))
        u = z[:, :d]
        v = z[:, d:]
        mu = jnp.mean(v, axis=-1, keepdims=True)
        vc = v - mu
        var = jnp.mean(vc * vc, axis=-1, keepdims=True)
        vn = (vc * lax.rsqrt(var + EPS) * lng_ref[...] + lnb_ref[...]).astype(BF16)
        nchunk = sub // SGU_CHUNK
        for hh in range(SGU_HEADS):
            cols = slice(hh * hd, (hh + 1) * hd)
            v_h = jnp.concatenate(
                [vn[n * SGU_CHUNK:(n + 1) * SGU_CHUNK, cols] for n in range(nchunk)], axis=1)
            mixed = jnp.dot(ws[hh], v_h, preferred_element_type=F32) + bs_ref[hh]
            for n in range(nchunk):
                rows = slice(n * SGU_CHUNK, (n + 1) * SGU_CHUNK)
                gated[s * sub + n * SGU_CHUNK:s * sub + (n + 1) * SGU_CHUNK, cols] = (
                    u[rows, cols] * mixed[:, n * hd:(n + 1) * hd]).astype(BF16)
        blk = slice(s * sub, (s + 1) * sub)
        o_ref[blk, :] = h_ref[blk, :] + jnp.dot(gated[blk, :], wout_ref[...],
                                                preferred_element_type=F32)


def _sgu(h, g, w_in, b_in, ln_g, ln_b, w_s, b_s, w_out):
    n, d = h.shape
    tm, sub = 1024, 256
    tile = lambda i: (i, 0)
    const2 = lambda i: (0, 0)
    const3 = lambda i: (0, 0, 0)
    return pl.pallas_call(
        functools.partial(_sgu_kernel, tm=tm, sub=sub),
        grid=(n // tm,),
        in_specs=[
            pl.BlockSpec((tm, d), tile),
            pl.BlockSpec((1, d), const2),
            pl.BlockSpec(w_in.shape, const2),
            pl.BlockSpec(b_in.shape, const2),
            pl.BlockSpec(ln_g.shape, const2),
            pl.BlockSpec(ln_b.shape, const2),
            pl.BlockSpec(w_s.shape, const3),
            pl.BlockSpec(b_s.shape, const3),
            pl.BlockSpec(w_out.shape, const2),
        ],
        out_specs=pl.BlockSpec((tm, d), tile),
        out_shape=jax.ShapeDtypeStruct((n, d), F32),
        scratch_shapes=[
            pltpu.VMEM((tm, d), BF16),
            pltpu.VMEM((2, sub, 2 * d), F32),
            pltpu.VMEM((tm, d), BF16),
        ],
        compiler_params=pltpu.CompilerParams(
            dimension_semantics=("arbitrary",), vmem_limit_bytes=VMEM_LIMIT),
        name="sgu",
    )(h, g, w_in, b_in, ln_g, ln_b, w_s, b_s, w_out)


def _route_rows(lt):
    neg = jnp.float32(-jnp.inf)

    def first_max(rows):
        m = functools.reduce(jnp.maximum, rows)
        idx = jnp.int32(len(rows))
        for r in reversed(range(len(rows))):
            idx = jnp.where(rows[r] == m, jnp.int32(r), idx)
        return m, idx

    gl = [lt[g:g + 1, :] for g in range(N_GROUPS)]
    gmax, gidx = first_max(gl)
    g_w = 1.0 / functools.reduce(jnp.add, [jnp.exp(x - gmax) for x in gl])
    el = []
    for j in range(EXPERTS_PER_GROUP):
        row = lt[N_GROUPS + j:N_GROUPS + j + 1, :]
        for g in range(1, N_GROUPS):
            r = N_GROUPS + EXPERTS_PER_GROUP * g + j
            row = jnp.where(gidx == g, lt[r:r + 1, :], row)
        el.append(row)
    m1, i1 = first_max(el)
    m2, i2 = first_max([jnp.where(i1 == j, neg, el[j]) for j in range(EXPERTS_PER_GROUP)])
    e2 = jnp.exp(m2 - m1)
    den = 1.0 + e2
    w1 = g_w / den
    w2 = g_w * e2 / den
    cw = [jnp.where(i1 == j, w1, 0.0) + jnp.where(i2 == j, w2, 0.0) for j in range(EXPERTS_PER_GROUP)]
    return gidx, cw


def _moe_kernel(h_ref, g_ref, wr_ref, br_ref, triu_ref, wg_ref, wu_ref, wd_ref, gf_ref,
                o_ref, xs_ref, ws_ref, ys_ref, pos_ref, hg_buf, hu_buf, act_ref, bounds, *, final_norm):
    step = pl.program_id(1)
    tm, d = h_ref.shape
    nperm = tm // PERM_BLOCK

    @pl.when(step == 0)
    def _():
        xn = _rms(h_ref[...], g_ref[...])
        x_hi = xn.astype(BF16)
        x_lo = (xn - x_hi.astype(F32)).astype(BF16)
        both = jnp.dot(x_hi, wr_ref[...], preferred_element_type=F32)
        logits = (both[:, :ROUTER_LANES] + both[:, ROUTER_LANES:]
                  + jnp.dot(x_lo, wr_ref[:, :ROUTER_LANES], preferred_element_type=F32)) + br_ref[...]
        gidx, cw = _route_rows(logits.T)

        row16 = lax.broadcasted_iota(jnp.int32, (ROW_ALIGN, tm), 0)
        onehot = jnp.where(row16 == gidx, 1.0, 0.0)
        rank = jnp.dot(onehot.astype(BF16), triu_ref[...], preferred_element_type=F32)
        pos_row = jnp.zeros((1, tm), F32)
        start = jnp.zeros((1, 1), F32)
        for g in range(N_GROUPS):
            bounds[g] = start[0, 0].astype(jnp.int32)
            mine = gidx == g
            pos_row = jnp.where(mine, rank[g:g + 1, :] + start, pos_row)
            start = start + jnp.sum(jnp.where(mine, 1.0, 0.0), axis=-1, keepdims=True)
        bounds[N_GROUPS] = jnp.int32(tm)

        packed = jnp.where(row16 == 0, pos_row, 0.0)
        for j in range(EXPERTS_PER_GROUP):
            piece = cw[j]
            for p in range(3):
                part = piece.astype(BF16).astype(F32)
                packed = packed + jnp.where(row16 == 1 + p * EXPERTS_PER_GROUP + j, part, 0.0)
                piece = piece - part
        packed = jnp.concatenate(
            [packed, jnp.zeros((ROUTER_LANES - ROW_ALIGN, tm), F32)], axis=0)
        tok = packed.T
        pos_ref[...] = tok
        w3 = tok.astype(BF16)
        pos_row = pos_row.astype(jnp.int32)

        for rb in range(nperm):
            rows = slice(rb * PERM_BLOCK, (rb + 1) * PERM_BLOCK)
            sorted_row = lax.broadcasted_iota(jnp.int32, (PERM_BLOCK, tm), 0) + rb * PERM_BLOCK
            perm = jnp.where(sorted_row == pos_row, 1.0, 0.0).astype(BF16)
            xs_ref[rows, :] = jnp.dot(perm, x_hi, preferred_element_type=F32).astype(BF16)
            ws_ref[rows, :] = jnp.dot(perm, w3, preferred_element_type=F32)
        ys_ref[...] = jnp.zeros_like(ys_ref)

    lane_b = lax.broadcasted_iota(jnp.int32, (MOE_BLOCK, ROUTER_LANES), 1)

    def expert_block(gg, seg_lo, seg_hi, blk_lo):
        row0 = pl.multiple_of(jnp.minimum(blk_lo, tm - MOE_BLOCK), ROW_ALIGN)
        rows = pl.ds(row0, MOE_BLOCK)
        sorted_row = lax.broadcasted_iota(jnp.int32, (MOE_BLOCK, 1), 0) + row0
        in_seg = ((sorted_row >= jnp.maximum(seg_lo, blk_lo))
                  & (sorted_row < jnp.minimum(seg_hi, blk_lo + MOE_BLOCK)))
        w3 = ws_ref[rows, :]

        def gate_up(j):
            e = gg * EXPERTS_PER_GROUP + j
            hg_buf[j % 2] = jnp.dot(xs_ref[rows, :], wg_ref[0, e], preferred_element_type=F32)
            hu_buf[j % 2] = jnp.dot(xs_ref[rows, :], wu_ref[0, e], preferred_element_type=F32)

        gate_up(0)
        for j in range(EXPERTS_PER_GROUP):
            if j + 1 < EXPERTS_PER_GROUP:
                gate_up(j + 1)
            mine = (((lane_b - 1) % EXPERTS_PER_GROUP == j) & (lane_b >= 1)
                    & (lane_b <= 3 * EXPERTS_PER_GROUP))
            c = jnp.sum(jnp.where(mine, w3, 0.0), axis=-1, keepdims=True)
            c = jnp.where(in_seg, c, 0.0)
            hg = hg_buf[j % 2]
            act = hg * (1.0 / (1.0 + jnp.exp(-hg))) * hu_buf[j % 2] * c
            act_ref[:, j * D_EXPERT:(j + 1) * D_EXPERT] = act.astype(BF16)
        ys_ref[rows, :] += jnp.dot(act_ref[...], wd_ref[0, gg], preferred_element_type=F32)

    for gg in range(MOE_GROUPS_PER_STEP):
        grp = step * MOE_GROUPS_PER_STEP + gg
        seg_lo = bounds[grp]
        seg_hi = bounds[grp + 1]
        first = (seg_lo // ROW_ALIGN) * ROW_ALIGN
        for k in range(pl.cdiv(tm, MOE_BLOCK)):
            blk_lo = first + k * MOE_BLOCK
            pl.when(blk_lo < seg_hi)(functools.partial(expert_block, gg, seg_lo, seg_hi, blk_lo))

    @pl.when(step == N_GROUPS // MOE_GROUPS_PER_STEP - 1)
    def _():
        xs_ref[...] = ys_ref[...].astype(BF16)
        for tb in range(nperm):
            rows = slice(tb * PERM_BLOCK, (tb + 1) * PERM_BLOCK)
            pos_i = pos_ref[rows, :1].astype(jnp.int32)
            perm_t = jnp.where(lax.broadcasted_iota(jnp.int32, (PERM_BLOCK, tm), 1) == pos_i,
                               1.0, 0.0).astype(BF16)
            out = h_ref[rows, :] + jnp.dot(perm_t, xs_ref[...], preferred_element_type=F32)
            if final_norm:
                out = _rms(out, gf_ref[...])
            o_ref[rows, :] = out


def _moe(h, g, wr, br, wg, wu, wd, gf, layer, final_norm):
    n, d = h.shape
    tm = MOE_TILE
    tile = lambda i, e: (i, 0)
    const2 = lambda i, e: (0, 0)
    group = lambda i, e: (layer, e, 0, 0)
    triu = jnp.tri(tm, tm, -1, dtype=BF16).T
    return pl.pallas_call(
        functools.partial(_moe_kernel, final_norm=final_norm),
        grid=(n // tm, N_GROUPS // MOE_GROUPS_PER_STEP),
        in_specs=[
            pl.BlockSpec((tm, d), tile),
            pl.BlockSpec((1, d), const2),
            pl.BlockSpec(wr.shape, const2),
            pl.BlockSpec(br.shape, const2),
            pl.BlockSpec(triu.shape, const2),
            pl.BlockSpec((1, MOE_GROUPS_PER_STEP * EXPERTS_PER_GROUP) + wg.shape[2:], group),
            pl.BlockSpec((1, MOE_GROUPS_PER_STEP * EXPERTS_PER_GROUP) + wu.shape[2:], group),
            pl.BlockSpec((1, MOE_GROUPS_PER_STEP) + wd.shape[2:], group),
            pl.BlockSpec((1, d), const2),
        ],
        out_specs=pl.BlockSpec((tm, d), tile),
        out_shape=jax.ShapeDtypeStruct((n, d), F32),
        scratch_shapes=[
            pltpu.VMEM((tm, d), BF16),
            pltpu.VMEM((tm, ROUTER_LANES), F32),
            pltpu.VMEM((tm, d), F32),
            pltpu.VMEM((tm, ROUTER_LANES), F32),
            pltpu.VMEM((2, MOE_BLOCK, D_EXPERT), F32),
            pltpu.VMEM((2, MOE_BLOCK, D_EXPERT), F32),
            pltpu.VMEM((MOE_BLOCK, EXPERTS_PER_GROUP * D_EXPERT), BF16),
            pltpu.SMEM((8,), jnp.int32),
        ],
        compiler_params=pltpu.CompilerParams(
            dimension_semantics=("arbitrary", "arbitrary"), vmem_limit_bytes=VMEM_LIMIT),
        name="hier_moe",
    )(h, g, wr, br, triu, wg, wu, wd, gf)


def _cast_kernel(x_ref, o_ref):
    o_ref[...] = x_ref[...].astype(o_ref.dtype)


def _to_bf16(w):
    l, e, a, b = w.shape
    blk = (1, EXPERTS_PER_GROUP, a, b)
    spec = pl.BlockSpec(blk, lambda i, j: (i, j, 0, 0))
    return pl.pallas_call(
        _cast_kernel,
        grid=(l, e // EXPERTS_PER_GROUP),
        in_specs=[spec],
        out_specs=spec,
        out_shape=jax.ShapeDtypeStruct(w.shape, BF16),
        compiler_params=pltpu.CompilerParams(
            dimension_semantics=("arbitrary", "arbitrary"), vmem_limit_bytes=VMEM_LIMIT),
        name="cast_bf16",
    )(w)


def _router_params(wg_router, bg_router, we_router, be_router):
    d = wg_router.shape[0]
    we = jnp.transpose(we_router, (1, 0, 2)).reshape(d, N_EXPERTS)
    w = jnp.concatenate([wg_router, we], axis=1)
    w = jnp.pad(w, ((0, 0), (0, ROUTER_LANES - w.shape[1])))
    bias = jnp.concatenate([bg_router, be_router.reshape(N_EXPERTS)])
    bias = jnp.pad(bias, (0, ROUTER_LANES - bias.shape[0])).reshape(1, ROUTER_LANES)
    w_hi = w.astype(BF16)
    w_lo = (w - w_hi.astype(F32)).astype(BF16)
    return jnp.concatenate([w_hi, w_lo], axis=1), bias.astype(F32)


def _band_bias_t(rel_bias):
    width = PAIR_BAND + PAIR - 1
    m = jnp.arange(width + 1) - (PAIR - 1) - LEFT
    ext = rel_bias.astype(F32)[:, jnp.clip(m, -MAX_REL, MAX_REL) + MAX_REL]
    skew = jnp.tile(ext, (1, PAIR))[:, :PAIR * width].reshape(ATT_HEADS, PAIR, width)
    bias = skew[:, :, PAIR - 1:PAIR - 1 + PAIR_BAND]
    first = (jnp.arange(PAIR) // CHUNK * CHUNK)[:, None]
    kk = jnp.arange(PAIR_BAND)[None, :]
    bias = jnp.where(((kk >= first) & (kk < first + BAND))[None], bias * LOG2E, -1e30)
    bias_t = jnp.transpose(bias, (0, 2, 1)).reshape(ATT_HEADS // 2, 2, PAIR_BAND, PAIR)
    return jnp.transpose(bias_t, (0, 2, 1, 3)).reshape(ATT_HEADS // 2, PAIR_BAND, 2 * PAIR)


def kernel(x, norm_mix_g, norm_ffn_g, final_norm_g, ab_w_in, pool_w, pool_scale, att_rel_bias,
           ab_w_out, sgu_w_in, sgu_b_in, sgu_ln_g, sgu_ln_b, sgu_w_s, sgu_b_s, sgu_w_out,
           moe_wg_router, moe_bg_router, moe_we_router, moe_be_router,
           moe_w_gate, moe_w_up, moe_w_down):
    b, s, d = x.shape
    row = lambda a: a.reshape(1, -1).astype(F32)

    wg_all = _to_bf16(moe_w_gate)
    wu_all = _to_bf16(moe_w_up)
    wd_all = _to_bf16(moe_w_down).reshape(-1, N_GROUPS, EXPERTS_PER_GROUP * D_EXPERT, d)

    def moe(h, layer, final_norm):
        wr, br = _router_params(moe_wg_router[layer], moe_bg_router[layer],
                                moe_we_router[layer], moe_be_router[layer])
        return _moe(h, row(norm_ffn_g[layer]), wr, br, wg_all, wu_all, wd_all, row(final_norm_g),
                    layer, final_norm)

    w_in = ab_w_in[0]
    n_pqk = POOL_WIDTH + 2 * ATT_WIDTH
    p, q, k, vt = _inproj(x, row(norm_mix_g[0]), w_in[:, :n_pqk].astype(BF16),
                          w_in[:, n_pqk:].T.astype(BF16))
    h = _mix(x, p, q, k, vt, _band_bias_t(att_rel_bias[0]), pool_w[0].astype(BF16),
             row(pool_scale[0]), ab_w_out[0].astype(BF16))
    h = moe(h.reshape(b * s, d), 0, False)

    h = _sgu(h, row(norm_mix_g[1]), sgu_w_in[0].astype(BF16), row(sgu_b_in[0]),
             row(sgu_ln_g[0]), row(sgu_ln_b[0]), sgu_w_s[0].astype(BF16),
             sgu_b_s[0].astype(F32)[:, :, None], sgu_w_out[0].astype(BF16))
    h = moe(h, 1, True)
    return h.reshape(b, s, d)
```
